```python
import math
import jax
import jax.numpy as jnp
from jax import lax
import numpy as np

D_MODEL = 1024
BATCH = 16
SEQ = 2048
DEPTH = 2

CTX_LEN = 256
GRID_W = 64
ROPE_THETA = 10000.0
NORM_EPS = 1e-6
QBLOCK = 128

HEAD_DIM = 64
GROUP_W = D_MODEL // 4
MIX_W = 4 * GROUP_W

A_HEADS = GROUP_W // HEAD_DIM
A_KV_HEADS = A_HEADS // 2
SSM_HEAD_DIM = 64
SSM_HEADS = GROUP_W // SSM_HEAD_DIM
SSM_STATE = 128
SSM_GROUPS = 2
SSM_CONV = 3
SSM_CHUNK = 64
SSM_CONV_CH = GROUP_W + 2 * SSM_GROUPS * SSM_STATE
C_HEADS = GROUP_W // HEAD_DIM
C_KV_HEADS = C_HEADS // 2
WINDOW = 128
MLA_HEADS = 4
MLA_V_DIM = GROUP_W // MLA_HEADS
MLA_NOPE = 64
MLA_ROPE = 32
MLA_Q_RANK = 192
MLA_KV_RANK = 128
D_FF = 2816
FFN_CONV = 3

A_COLS = (A_HEADS + 2 * A_KV_HEADS) * HEAD_DIM
B_COLS = GROUP_W + SSM_CONV_CH + 2 * SSM_HEADS
C_COLS = (C_HEADS + 2 * C_KV_HEADS) * HEAD_DIM
D_COLS = MLA_Q_RANK + MLA_KV_RANK + MLA_ROPE
IN_COLS = A_COLS + B_COLS + C_COLS + D_COLS
IN_SPLITS = [A_COLS, A_COLS + B_COLS, A_COLS + B_COLS + C_COLS]

kernel_name = 'hybrid_parallel_group_dit_block'


def rmsnorm(x, w):
    xf = x.astype(jnp.float32)
    y = xf * lax.rsqrt(jnp.mean(xf * xf, axis=-1, keepdims=True) + NORM_EPS)
    return (y * w.astype(jnp.float32)).astype(x.dtype)


def axial_rope_tables(n, rot_dim):
    rows = n // GRID_W
    row = jnp.repeat(jnp.arange(rows, dtype=jnp.float32), GRID_W)
    col = jnp.tile(jnp.arange(GRID_W, dtype=jnp.float32), rows)
    half = rot_dim // 2
    inv_freq = jnp.power(ROPE_THETA, -jnp.arange(0, half, 2, dtype=jnp.float32) / half)
    ang_r = row[:, None] * inv_freq[None, :]
    ang_c = col[:, None] * inv_freq[None, :]
    ang = jnp.concatenate([ang_r, ang_r, ang_c, ang_c], axis=-1)
    return jnp.cos(ang), jnp.sin(ang)


def rotate_half(u):
    u1, u2 = jnp.split(u, 2, axis=-1)
    return jnp.concatenate([-u2, u1], axis=-1)


def apply_axial_rope(x, cos, sin):
    bshape = (1, cos.shape[0]) + (1,) * (x.ndim - 3) + (cos.shape[-1],)
    cos = cos.reshape(bshape)
    sin = sin.reshape(bshape)
    xf = x.astype(jnp.float32)
    x_row, x_col = jnp.split(xf, 2, axis=-1)
    rot = jnp.concatenate([rotate_half(x_row), rotate_half(x_col)], axis=-1)
    return (xf * cos + rot * sin).astype(x.dtype)


def dwconv_centred(u, w, b):
    k_w = w.shape[1]
    pad = k_w // 2
    t_len = u.shape[1]
    up = jnp.pad(u, ((0, 0), (pad, pad), (0, 0)))
    out = b
    for k in range(k_w):
        out = out + up[:, k:k + t_len, :] * w[:, k]
    return out


def softmax_sink(s, sink):
    m = jnp.maximum(jnp.max(s, axis=-1, keepdims=True), sink)
    e = jnp.exp(s - m)
    return e / (jnp.sum(e, axis=-1, keepdims=True) + jnp.exp(sink - m))


def attend(q, k, v, scale, sink=None):
    s = jnp.einsum('bqkgd,bnkd->bkgqn', q, k).astype(jnp.float32) * scale
    if sink is None:
        p = jax.nn.softmax(s, axis=-1)
    else:
        p = softmax_sink(s, sink)
    return jnp.einsum('bkgqn,bnkd->bqkgd', p.astype(v.dtype), v)


def dense_latent_attention(q, k_all, v_all, scale):
    b_sz, s_len = q.shape[:2]
    nb = s_len // QBLOCK
    qb = jnp.moveaxis(q.reshape((b_sz, nb, QBLOCK) + q.shape[2:]), 1, 0)
    out = lax.map(lambda blk: attend(blk, k_all, v_all, scale), qb)
    return jnp.moveaxis(out, 0, 1).reshape(b_sz, s_len, -1)


def split_gqa(p, n_heads, n_kv):
    b_sz, t_len = p.shape[:2]
    q, k, v = jnp.split(p, [n_heads * HEAD_DIM, (n_heads + n_kv) * HEAD_DIM], axis=-1)
    q = q.reshape(b_sz, t_len, n_kv, n_heads // n_kv, HEAD_DIM)
    k = k.reshape(b_sz, t_len, n_kv, HEAD_DIM)
    v = v.reshape(b_sz, t_len, n_kv, HEAD_DIM)
    return q, k, v


def mixer_gqa(p, pc, q_norm, k_norm, cos, sin, with_ctx):
    ql, kl, vl = split_gqa(p, A_HEADS, A_KV_HEADS)
    qc, kc, vc = split_gqa(pc, A_HEADS, A_KV_HEADS)
    ql = apply_axial_rope(rmsnorm(ql, q_norm), cos, sin)
    kl = apply_axial_rope(rmsnorm(kl, k_norm), cos, sin)
    kc = rmsnorm(kc, k_norm)
    scale = HEAD_DIM ** -0.5
    k_all = jnp.concatenate([kl, kc], axis=1)
    v_all = jnp.concatenate([vl, vc], axis=1)
    y = dense_latent_attention(ql, k_all, v_all, scale)
    yc = None
    if with_ctx:
        qc = rmsnorm(qc, q_norm)
        yc = attend(qc, kc, vc, scale).reshape(pc.shape[0], pc.shape[1], GROUP_W)
    return y, yc


def ssd_scan(xh, dt, a, bm, cm, h0):
    b_sz, t_len, n_h, p_dim = xh.shape
    n_st = bm.shape[-1]
    L = SSM_CHUNK
    nc = t_len // L
    xr = (xh * dt[..., None]).reshape(b_sz, nc, L, n_h, p_dim)
    br = bm.reshape(b_sz, nc, L, n_h, n_st)
    cr = cm.reshape(b_sz, nc, L, n_h, n_st)
    cs = jnp.cumsum(jnp.moveaxis((dt * a).reshape(b_sz, nc, L, n_h), 3, 1), axis=-1)
    seg = cs[..., :, None] - cs[..., None, :]
    lower = jnp.tril(jnp.ones((L, L), dtype=bool))
    decay = jnp.exp(jnp.where(lower, seg, -jnp.inf))
    scores = jnp.einsum('bclhn,bcshn->bhcls', cr, br) * decay
    y_diag = jnp.einsum('bhcls,bcshp->bclhp', scores, xr)
    w_state = jnp.exp(cs[..., -1:] - cs)
    states = jnp.einsum('bclhn,bhcl,bclhp->bchpn', br, w_state, xr)
    chunk_decay = jnp.exp(cs[..., -1])

    def step(h, inp):
        s_c, d_c = inp
        return h * d_c[..., None, None] + s_c, h

    h_final, h_in = lax.scan(step, h0, (jnp.moveaxis(states, 1, 0), jnp.moveaxis(chunk_decay, 2, 0)))
    h_in = jnp.moveaxis(h_in, 0, 1)
    y_off = jnp.einsum('bclhn,bchpn,bhcl->bclhp', cr, h_in, jnp.exp(cs))
    return (y_diag + y_off).reshape(b_sz, t_len, n_h, p_dim), h_final


def mixer_ssd(p, pc, conv_w, conv_b, dt_bias, a_log, d_skip, norm_w, with_ctx):
    rep = SSM_HEADS // SSM_GROUPS

    def prep(u):
        b_sz, t_len = u.shape[:2]
        z, xbc, dtr = jnp.split(u, [GROUP_W, GROUP_W + SSM_CONV_CH], axis=-1)
        xbc = jax.nn.silu(dwconv_centred(xbc, conv_w, conv_b))
        xs, bm, cm = jnp.split(xbc, [GROUP_W, GROUP_W + SSM_GROUPS * SSM_STATE], axis=-1)
        xs = xs.reshape(b_sz, t_len, SSM_HEADS, SSM_HEAD_DIM).astype(jnp.float32)
        bm = jnp.repeat(bm.reshape(b_sz, t_len, SSM_GROUPS, SSM_STATE), rep, axis=2).astype(jnp.float32)
        cm = jnp.repeat(cm.reshape(b_sz, t_len, SSM_GROUPS, SSM_STATE), rep, axis=2).astype(jnp.float32)
        dt = jax.nn.softplus(dtr.reshape(b_sz, t_len, 2, SSM_HEADS).astype(jnp.float32)
                             + dt_bias.astype(jnp.float32))
        return z, xs, bm, cm, dt

    flip = lambda t: jnp.flip(t, axis=1)
    a = -jnp.exp(a_log.astype(jnp.float32))
    zc, xc, bc, cc, dtc = prep(pc)
    zl, xl, bl, cl, dtl = prep(p)
    h0 = jnp.zeros((pc.shape[0], SSM_HEADS, SSM_HEAD_DIM, SSM_STATE), jnp.float32)
    yc_f, hc_f = ssd_scan(xc, dtc[:, :, 0], a[0], bc, cc, h0)
    yc_b, hc_b = ssd_scan(flip(xc), flip(dtc[:, :, 1]), a[1], flip(bc), flip(cc), h0)
    yl_f, _ = ssd_scan(xl, dtl[:, :, 0], a[0], bl, cl, hc_f)
    yl_b, _ = ssd_scan(flip(xl), flip(dtl[:, :, 1]), a[1], flip(bl), flip(cl), hc_b)
    d_f = d_skip.astype(jnp.float32)[:, None]

    def finish(y_f, y_b_rev, xs, z):
        y = y_f + flip(y_b_rev) + xs * d_f
        y = y.reshape(z.shape[0], z.shape[1], GROUP_W).astype(z.dtype)
        return rmsnorm(y * jax.nn.silu(z), norm_w)

    y = finish(yl_f, yl_b, xl, zl)
    yc = finish(yc_f, yc_b, xc, zc) if with_ctx else None
    return y, yc


def band_blocks(t, nb):
    tb = t.reshape((t.shape[0], nb, QBLOCK) + t.shape[2:])
    tp = jnp.pad(tb, ((0, 0), (1, 1), (0, 0), (0, 0), (0, 0)))
    return jnp.concatenate([tp[:, :-2], tp[:, 1:-1], tp[:, 2:]], axis=2)


def mixer_window(p, pc, sink, cos, sin, with_ctx):
    ql, kl, vl = split_gqa(p, C_HEADS, C_KV_HEADS)
    qc, kc, vc = split_gqa(pc, C_HEADS, C_KV_HEADS)
    ql = apply_axial_rope(ql, cos, sin)
    kl = apply_axial_rope(kl, cos, sin)
    b_sz, s_len = ql.shape[:2]
    nb = s_len // QBLOCK
    n_g = C_HEADS // C_KV_HEADS
    scale = HEAD_DIM ** -0.5
    sk = sink.astype(jnp.float32).reshape(C_KV_HEADS, n_g, 1, 1)
    qb = ql.reshape(b_sz, nb, QBLOCK, C_KV_HEADS, n_g, HEAD_DIM)
    kb = band_blocks(kl, nb)
    vb = band_blocks(vl, nb)
    blk = jnp.arange(nb)[:, None]
    qpos = blk * QBLOCK + jnp.arange(QBLOCK)[None, :]
    kpos = (blk - 1) * QBLOCK + jnp.arange(3 * QBLOCK)[None, :]
    kp = kpos[:, None, :]
    mask = (jnp.abs(kp - qpos[:, :, None]) <= WINDOW) & (kp >= 0) & (kp < s_len)
    s_band = jnp.einsum('bnqkgd,bnrkd->bnkgqr', qb, kb).astype(jnp.float32) * scale
    s_band = jnp.where(mask[None, :, None, None], s_band, -jnp.inf)
    s_ctx = jnp.einsum('bnqkgd,bckd->bnkgqc', qb, kc).astype(jnp.float32) * scale
    prob = softmax_sink(jnp.concatenate([s_band, s_ctx], axis=-1), sk).astype(vl.dtype)
    y = (jnp.einsum('bnkgqr,bnrkd->bnqkgd', prob[..., :3 * QBLOCK], vb)
         + jnp.einsum('bnkgqc,bckd->bnqkgd', prob[..., 3 * QBLOCK:], vc))
    y = y.reshape(b_sz, s_len, GROUP_W)
    yc = None
    if with_ctx:
        yc = attend(qc, kc, vc, scale, sk).reshape(pc.shape[0], pc.shape[1], GROUP_W)
    return y, yc


def mla_heads(p, q_norm, w_uq, kv_norm, w_ukv, cos, sin, use_rope):
    b_sz, t_len = p.shape[:2]
    cq, ckv, k_rot = jnp.split(p, [MLA_Q_RANK, MLA_Q_RANK + MLA_KV_RANK], axis=-1)
    q = (rmsnorm(cq, q_norm) @ w_uq).reshape(b_sz, t_len, MLA_HEADS, MLA_NOPE + MLA_ROPE)
    kv = (rmsnorm(ckv, kv_norm) @ w_ukv).reshape(b_sz, t_len, MLA_HEADS, MLA_NOPE + MLA_V_DIM)
    q_nope, q_rot = jnp.split(q, [MLA_NOPE], axis=-1)
    k_nope, v = jnp.split(kv, [MLA_NOPE], axis=-1)
    k_rot = k_rot[:, :, None, :]
    if use_rope:
        q_rot = apply_axial_rope(q_rot, cos, sin)
        k_rot = apply_axial_rope(k_rot, cos, sin)
    k = jnp.concatenate([k_nope, jnp.broadcast_to(k_rot, k_nope.shape[:-1] + (MLA_ROPE,))], axis=-1)
    q = jnp.concatenate([q_nope, q_rot], axis=-1)
    return q[:, :, :, None, :], k, v


def mixer_mla(p, pc, q_norm, w_uq, kv_norm, w_ukv, cos, sin, with_ctx):
    ql, kl, vl = mla_heads(p, q_norm, w_uq, kv_norm, w_ukv, cos, sin, True)
    qc, kc, vc = mla_heads(pc, q_norm, w_uq, kv_norm, w_ukv, cos, sin, False)
    scale = (MLA_NOPE + MLA_ROPE) ** -0.5
    k_all = jnp.concatenate([kl, kc], axis=1)
    v_all = jnp.concatenate([vl, vc], axis=1)
    y = dense_latent_attention(ql, k_all, v_all, scale)
    yc = attend(qc, kc, vc, scale).reshape(pc.shape[0], pc.shape[1], GROUP_W) if with_ctx else None
    return y, yc


def conv_glu(h, w_up, conv_w, conv_b, w_down):
    a, g = jnp.split(h @ w_up, 2, axis=-1)
    g = dwconv_centred(g, conv_w, conv_b)
    return (a * jax.nn.silu(g)) @ w_down


def setup_inputs(seed: int = 0) -> dict:
    key = jax.random.key(seed)
    ks = jax.random.split(key, 28)
    f32 = jnp.float32

    def nrm(i, shape, scale):
        return jax.random.normal(ks[i], shape, f32) * scale

    def gain(i, shape):
        return 1.0 + 0.05 * jax.random.normal(ks[i], shape, f32)

    L = DEPTH
    dt0 = jnp.exp(jax.random.uniform(ks[12], (L, 2, SSM_HEADS), f32, math.log(1e-3), math.log(1e-1)))
    dt_bias = dt0 + jnp.log(-jnp.expm1(-dt0))
    a_log = jnp.log(jax.random.uniform(ks[13], (L, 2, SSM_HEADS), f32, 1.0, 16.0))
    return {
        'x': nrm(0, (BATCH, SEQ, D_MODEL), 1.0),
        'c': nrm(1, (BATCH, D_MODEL), 1.0),
        'ctx': nrm(2, (BATCH, CTX_LEN, D_MODEL), 1.0),
        'c_ctx': nrm(3, (D_MODEL,), 1.0),
        'norm1_w': gain(4, (L, D_MODEL)),
        'w_mod': nrm(5, (L, D_MODEL, 6 * D_MODEL), D_MODEL ** -0.5),
        'b_mod': nrm(6, (L, 6 * D_MODEL), 0.02),
        'w_in': nrm(7, (L, D_MODEL, IN_COLS), D_MODEL ** -0.5),
        'attn_q_norm': gain(8, (L, HEAD_DIM)),
        'attn_k_norm': gain(9, (L, HEAD_DIM)),
        'ssm_conv_w': nrm(10, (L, SSM_CONV_CH, SSM_CONV), SSM_CONV ** -0.5),
        'ssm_conv_b': nrm(11, (L, SSM_CONV_CH), 0.02),
        'ssm_dt_bias': dt_bias,
        'ssm_a_log': a_log,
        'ssm_d': gain(14, (L, SSM_HEADS)),
        'ssm_norm_w': gain(15, (L, GROUP_W)),
        'win_sink': nrm(16, (L, C_HEADS), 0.5),
        'mla_q_norm': gain(17, (L, MLA_Q_RANK)),
        'mla_w_uq': nrm(18, (L, MLA_Q_RANK, MLA_HEADS * (MLA_NOPE + MLA_ROPE)), MLA_Q_RANK ** -0.5),
        'mla_kv_norm': gain(19, (L, MLA_KV_RANK)),
        'mla_w_ukv': nrm(20, (L, MLA_KV_RANK, MLA_HEADS * (MLA_NOPE + MLA_V_DIM)), MLA_KV_RANK ** -0.5),
        'w_out': nrm(21, (L, MIX_W, D_MODEL), MIX_W ** -0.5),
        'norm2_w': gain(22, (L, D_MODEL)),
        'ffn_w_up': nrm(23, (L, D_MODEL, 2 * D_FF), D_MODEL ** -0.5),
        'ffn_conv_w': nrm(24, (L, D_FF, FFN_CONV), FFN_CONV ** -0.5),
        'ffn_conv_b': nrm(25, (L, D_FF), 0.02),
        'ffn_w_down': nrm(26, (L, D_FF, D_MODEL), D_FF ** -0.5),
        'final_norm_w': gain(27, (D_MODEL,)),
    }


def reference(x, c, ctx, c_ctx, norm1_w, w_mod, b_mod, w_in, attn_q_norm, attn_k_norm,
              ssm_conv_w, ssm_conv_b, ssm_dt_bias, ssm_a_log, ssm_d, ssm_norm_w, win_sink,
              mla_q_norm, mla_w_uq, mla_kv_norm, mla_w_ukv, w_out, norm2_w,
              ffn_w_up, ffn_conv_w, ffn_conv_b, ffn_w_down, final_norm_w):
    s_len = x.shape[1]
    cos_a, sin_a = axial_rope_tables(s_len, HEAD_DIM)
    cos_m, sin_m = axial_rope_tables(s_len, MLA_ROPE)
    c_act = jax.nn.silu(c)
    cc_act = jax.nn.silu(c_ctx)
    xc = ctx
    for l in range(DEPTH):
        with_ctx = l < DEPTH - 1
        mod = c_act @ w_mod[l] + b_mod[l]
        mod_c = cc_act @ w_mod[l] + b_mod[l]
        sh1, sc1, g1, sh2, sc2, g2 = jnp.split(mod[:, None, :], 6, axis=-1)
        csh1, csc1, cg1, csh2, csc2, cg2 = jnp.split(mod_c, 6, axis=-1)

        h = rmsnorm(x, norm1_w[l]) * (1 + sc1) + sh1
        hc = rmsnorm(xc, norm1_w[l]) * (1 + csc1) + csh1
        la, lb, lw, lm = jnp.split(h @ w_in[l], IN_SPLITS, axis=-1)
        ca, cb, cw, cm = jnp.split(hc @ w_in[l], IN_SPLITS, axis=-1)
        ya, yca = mixer_gqa(la, ca, attn_q_norm[l], attn_k_norm[l], cos_a, sin_a, with_ctx)
        yb, ycb = mixer_ssd(lb, cb, ssm_conv_w[l], ssm_conv_b[l], ssm_dt_bias[l], ssm_a_log[l],
                            ssm_d[l], ssm_norm_w[l], with_ctx)
        yw, ycw = mixer_window(lw, cw, win_sink[l], cos_a, sin_a, with_ctx)
        ym, ycm = mixer_mla(lm, cm, mla_q_norm[l], mla_w_uq[l], mla_kv_norm[l], mla_w_ukv[l],
                            cos_m, sin_m, with_ctx)
        x = x + g1 * (jnp.concatenate([ya, yb, yw, ym], axis=-1) @ w_out[l])
        x = x + g2 * conv_glu(rmsnorm(x, norm2_w[l]) * (1 + sc2) + sh2,
                              ffn_w_up[l], ffn_conv_w[l], ffn_conv_b[l], ffn_w_down[l])
        if with_ctx:
            xc = xc + cg1 * (jnp.concatenate([yca, ycb, ycw, ycm], axis=-1) @ w_out[l])
            xc = xc + cg2 * conv_glu(rmsnorm(xc, norm2_w[l]) * (1 + csc2) + csh2,
                                     ffn_w_up[l], ffn_conv_w[l], ffn_conv_b[l], ffn_w_down[l])
    return rmsnorm(x, final_norm_w)
```

```python
import functools
import math

import numpy as np
import jax
import jax.numpy as jnp
from jax import lax
from jax.experimental import pallas as pl
from jax.experimental.pallas import tpu as pltpu

F32 = jnp.float32
BF16 = jnp.bfloat16

D_MODEL = 1024
SEQ = 2048
CTX_LEN = 256
TT = SEQ + CTX_LEN
GRID_W = 64
ROPE_THETA = 10000.0
NORM_EPS = 1e-6
HEAD_DIM = 64
GROUP_W = 256
WINDOW = 128
SSM_STATE = 128
SSM_CONV_CH = 768
MLA_NOPE = 64
MLA_ROPE = 32
MLA_Q_RANK = 192
MLA_KV_RANK = 128
D_FF = 2816

LANES = 128
BF16_ROWS = 16
TOK_TILE = 256
N_LAT_TILES = SEQ // TOK_TILE
N_TILES = TT // TOK_TILE
SSD_CHUNK = 128
FFN_CHUNK = 256
FFN_HALO = 8

PA_W, PB_W, PDT_W, PC_W, PD_W = 512, 1024, 128, 512, 512
OFF_A = 0
OFF_B = OFF_A + PA_W
OFF_DT = OFF_B + PB_W
OFF_C = OFF_DT + PDT_W
OFF_D = OFF_C + PC_W
PROJ_W = OFF_D + PD_W


def _resident(shape):
    return pl.BlockSpec(shape, lambda *_: (0,) * len(shape), pipeline_mode=pl.Buffered(1))


def _params(vmem_mb, ndims=2):
    return pltpu.CompilerParams(
        dimension_semantics=("arbitrary",) * ndims,
        vmem_limit_bytes=vmem_mb * 1024 * 1024,
    )


def _sigmoid(v):
    return 1.0 / (1.0 + jnp.exp(-v))


def _dot(a, b):
    return jnp.dot(a, b, preferred_element_type=F32)


def _dot_nt(a, b):
    return lax.dot_general(a, b, (((1,), (1,)), ((), ())), preferred_element_type=F32)


def _split3(v):
    t1 = v.astype(BF16)
    r1 = v - t1.astype(F32)
    t2 = r1.astype(BF16)
    t3 = (r1 - t2.astype(F32)).astype(BF16)
    return t1, t2, t3


def _mod_kernel(c_ref, w_ref, b_ref, o_ref):
    a = c_ref[...]
    s = a * _sigmoid(a)
    w = w_ref[...]
    s_hi = s.astype(BF16)
    s_lo = (s - s_hi.astype(F32)).astype(BF16)
    w_hi = w.astype(BF16)
    w_lo = (w - w_hi.astype(F32)).astype(BF16)
    o_ref[...] = _dot(s_hi, w_hi) + _dot(s_hi, w_lo) + _dot(s_lo, w_hi) + b_ref[...]


def _mod_call(cvec, w_mod, b_mod):
    n_layers, d, n6 = w_mod.shape
    rows = cvec.shape[0]
    tn = 1024
    return pl.pallas_call(
        _mod_kernel,
        grid=(n_layers, n6 // tn),
        in_specs=[
            pl.BlockSpec((rows, d), lambda l, j: (0, 0)),
            pl.BlockSpec((None, d, tn), lambda l, j: (l, 0, j)),
            pl.BlockSpec((None, 1, tn), lambda l, j: (l, 0, j)),
        ],
        out_specs=pl.BlockSpec((None, rows, tn), lambda l, j: (l, 0, j)),
        out_shape=jax.ShapeDtypeStruct((n_layers, rows, n6), F32),
        compiler_params=_params(40),
        name="mod",
    )(cvec, w_mod, b_mod)


def _proj_kernel(xl_ref, xc_ref, nw_ref, m_ref, w_ref, pa_ref, pb_ref, pdt_ref, pc_ref, pd_ref):
    i = pl.program_id(1)
    x = jnp.where(i < N_LAT_TILES, xl_ref[...], xc_ref[...])
    ms = jnp.mean(x * x, axis=-1, keepdims=True)
    y = x * lax.rsqrt(ms + NORM_EPS) * nw_ref[...]
    h = (y * (1.0 + m_ref[1:2, :]) + m_ref[0:1, :]).astype(BF16)
    pa_ref[...] = _dot(h, w_ref[:, OFF_A:OFF_A + PA_W]).astype(BF16)
    pb_ref[...] = _dot(h, w_ref[:, OFF_B:OFF_B + PB_W]).astype(BF16)
    pdt_ref[...] = _dot(h, w_ref[:, OFF_DT:OFF_DT + PDT_W])
    pc_ref[...] = _dot(h, w_ref[:, OFF_C:OFF_C + PC_W]).astype(BF16)
    pd_ref[...] = _dot(h, w_ref[:, OFF_D:OFF_D + PD_W]).astype(BF16)


def _proj_call(x_lat, x_ctx, norm_w, mods, w):
    b = x_lat.shape[0]
    tok = lambda width: pl.BlockSpec((None, TOK_TILE, width), lambda bi, i: (bi, i, 0))
    return pl.pallas_call(
        _proj_kernel,
        grid=(b, N_TILES),
        in_specs=[
            pl.BlockSpec((None, TOK_TILE, D_MODEL), lambda bi, i: (bi, jnp.minimum(i, N_LAT_TILES - 1), 0)),
            pl.BlockSpec((None, TOK_TILE, D_MODEL), lambda bi, i: (bi, 0, 0)),
            _resident((1, D_MODEL)),
            pl.BlockSpec((None, None, 8, D_MODEL), lambda bi, i: (bi, i // N_LAT_TILES, 0, 0)),
            _resident((D_MODEL, PROJ_W)),
        ],
        out_specs=[tok(PA_W), tok(PB_W), tok(PDT_W), tok(PC_W), tok(PD_W)],
        out_shape=[
            jax.ShapeDtypeStruct((b, TT, PA_W), BF16),
            jax.ShapeDtypeStruct((b, TT, PB_W), BF16),
            jax.ShapeDtypeStruct((b, TT, PDT_W), F32),
            jax.ShapeDtypeStruct((b, TT, PC_W), BF16),
            jax.ShapeDtypeStruct((b, TT, PD_W), BF16),
        ],
        compiler_params=_params(40),
        name="proj",
    )(x_lat, x_ctx, norm_w, mods, w)


def _rope(x, cos, sin_signed, half):
    lane = lax.broadcasted_iota(jnp.int32, x.shape, 1)
    first = (lane & (2 * half - 1)) < half
    partner = jnp.where(first, pltpu.roll(x, LANES - half, 1), pltpu.roll(x, half, 1))
    return x * cos + partner * sin_signed


def _axial_tables(rot_dim, lane_lo, reps):
    half = rot_dim // 2
    t = np.arange(SEQ)
    row = (t // GRID_W).astype(np.float32)
    col = (t % GRID_W).astype(np.float32)
    inv_freq = np.power(np.float32(ROPE_THETA), -np.arange(0, half, 2, dtype=np.float32) / np.float32(half))
    inv_freq = inv_freq.astype(np.float32)
    ang_r = (row[:, None] * inv_freq[None, :]).astype(np.float32)
    ang_c = (col[:, None] * inv_freq[None, :]).astype(np.float32)
    ang = np.concatenate([ang_r, ang_r, ang_c, ang_c], axis=-1).astype(np.float64)
    cos = np.cos(ang)
    sin = np.sin(ang)
    quarter = half // 2
    sign = np.where((np.arange(rot_dim) % half) < quarter, -1.0, 1.0)
    cos_t = np.ones((TT, LANES), np.float64)
    sin_t = np.zeros((TT, LANES), np.float64)
    for r in range(reps):
        lo = lane_lo + r * rot_dim
        cos_t[:SEQ, lo:lo + rot_dim] = cos
        sin_t[:SEQ, lo:lo + rot_dim] = sin * sign[None, :]
    return jnp.asarray(cos_t, F32), jnp.asarray(sin_t, F32)


def _prep_gqa_kernel(p_ref, cos_ref, sin_ref, qn_ref, kn_ref, q_ref, kt_ref, v_ref, *, do_norm):
    cos = cos_ref[...]
    sin = sin_ref[...]
    lane = lax.broadcasted_iota(jnp.int32, (TOK_TILE, LANES), 1)
    lo = lane < HEAD_DIM

    def head_norm(v, gain):
        v2 = v * v
        s_lo = jnp.sum(jnp.where(lo, v2, 0.0), axis=-1, keepdims=True)
        s_hi = jnp.sum(jnp.where(lo, 0.0, v2), axis=-1, keepdims=True)
        ms = jnp.where(lo, s_lo, s_hi) * (1.0 / HEAD_DIM)
        return v * lax.rsqrt(ms + NORM_EPS) * gain

    scale = HEAD_DIM ** -0.5
    for j in range(2):
        v = p_ref[:, j * LANES:(j + 1) * LANES].astype(F32)
        if do_norm:
            v = head_norm(v, qn_ref[...])
        v = _rope(v, cos, sin, HEAD_DIM // 4) * scale
        q_ref[:, (2 * j) * LANES:(2 * j + 1) * LANES] = jnp.where(lo, v, 0.0).astype(BF16)
        q_ref[:, (2 * j + 1) * LANES:(2 * j + 2) * LANES] = jnp.where(lo, pltpu.roll(v, HEAD_DIM, 1), 0.0).astype(BF16)

    k = p_ref[:, 2 * LANES:3 * LANES].astype(F32)
    if do_norm:
        k = head_norm(k, kn_ref[...])
    k = _rope(k, cos, sin, HEAD_DIM // 4)
    kt_ref[0] = jnp.where(lo, k, 0.0).T.astype(BF16)
    kt_ref[1] = jnp.where(lo, pltpu.roll(k, HEAD_DIM, 1), 0.0).T.astype(BF16)

    vv = p_ref[:, 3 * LANES:4 * LANES]
    v_ref[0] = vv
    v_ref[1] = pltpu.roll(vv.astype(F32), HEAD_DIM, 1).astype(BF16)


def _prep_gqa_call(p, cos, sin, qn, kn, do_norm, name):
    b = p.shape[0]
    return pl.pallas_call(
        functools.partial(_prep_gqa_kernel, do_norm=do_norm),
        grid=(b, N_TILES),
        in_specs=[
            pl.BlockSpec((None, TOK_TILE, PA_W), lambda bi, i: (bi, i, 0)),
            pl.BlockSpec((TOK_TILE, LANES), lambda bi, i: (i, 0)),
            pl.BlockSpec((TOK_TILE, LANES), lambda bi, i: (i, 0)),
            _resident((1, LANES)),
            _resident((1, LANES)),
        ],
        out_specs=[
            pl.BlockSpec((None, TOK_TILE, 4 * LANES), lambda bi, i: (bi, i, 0)),
            pl.BlockSpec((None, 2, LANES, TOK_TILE), lambda bi, i: (bi, 0, 0, i)),
            pl.BlockSpec((None, 2, TOK_TILE, LANES), lambda bi, i: (bi, 0, i, 0)),
        ],
        out_shape=[
            jax.ShapeDtypeStruct((b, TT, 4 * LANES), BF16),
            jax.ShapeDtypeStruct((b, 2, LANES, TT), BF16),
            jax.ShapeDtypeStruct((b, 2, TT, LANES), BF16),
        ],
        compiler_params=_params(32),
        name=name,
    )(p, cos, sin, qn, kn)


def _prep_mla_kernel(p_ref, cos_ref, sin_ref, gq_ref, gkv_ref, wq_ref, wkv_ref, q_ref, kt_ref, v_ref):
    cos = cos_ref[...]
    sin = sin_ref[...]
    half = MLA_ROPE // 4
    cq = p_ref[:, 0:2 * LANES].astype(F32)
    ms = jnp.sum(cq * cq, axis=-1, keepdims=True) * (1.0 / MLA_Q_RANK)
    cqn = (cq * lax.rsqrt(ms + NORM_EPS) * gq_ref[...]).astype(BF16)
    q = _dot(cqn, wq_ref[...])
    scale = (MLA_NOPE + MLA_ROPE) ** -0.5
    for h in range(4):
        qh = _rope(q[:, h * LANES:(h + 1) * LANES], cos, sin, half) * scale
        q_ref[:, h * LANES:(h + 1) * LANES] = qh.astype(BF16)

    ckv = p_ref[:, 2 * LANES:3 * LANES].astype(F32)
    ms = jnp.mean(ckv * ckv, axis=-1, keepdims=True)
    ckvn = (ckv * lax.rsqrt(ms + NORM_EPS) * gkv_ref[...]).astype(BF16)
    kv = _dot(ckvn, wkv_ref[...])
    kr = _rope(p_ref[:, 3 * LANES:4 * LANES].astype(F32), cos, sin, half)
    for h in range(4):
        kt_ref[h] = (kv[:, h * LANES:(h + 1) * LANES] + kr).T.astype(BF16)
    v_ref[0] = kv[:, 4 * LANES:5 * LANES].astype(BF16)
    v_ref[1] = kv[:, 5 * LANES:6 * LANES].astype(BF16)


def _prep_mla_call(p, cos, sin, gq, gkv, wq, wkv):
    b = p.shape[0]
    return pl.pallas_call(
        _prep_mla_kernel,
        grid=(b, N_TILES),
        in_specs=[
            pl.BlockSpec((None, TOK_TILE, PD_W), lambda bi, i: (bi, i, 0)),
            pl.BlockSpec((TOK_TILE, LANES), lambda bi, i: (i, 0)),
            pl.BlockSpec((TOK_TILE, LANES), lambda bi, i: (i, 0)),
            _resident((1, 2 * LANES)),
            _resident((1, LANES)),
            _resident((2 * LANES, 4 * LANES)),
            _resident((LANES, 6 * LANES)),
        ],
        out_specs=[
            pl.BlockSpec((None, TOK_TILE, 4 * LANES), lambda bi, i: (bi, i, 0)),
            pl.BlockSpec((None, 4, LANES, TOK_TILE), lambda bi, i: (bi, 0, 0, i)),
            pl.BlockSpec((None, 2, TOK_TILE, LANES), lambda bi, i: (bi, 0, i, 0)),
        ],
        out_shape=[
            jax.ShapeDtypeStruct((b, TT, 4 * LANES), BF16),
            jax.ShapeDtypeStruct((b, 4, LANES, TT), BF16),
            jax.ShapeDtypeStruct((b, 2, TT, LANES), BF16),
        ],
        compiler_params=_params(32),
        name="prep_mla",
    )(p, cos, sin, gq, gkv, wq, wkv)


def _softmax_pv(s_parts, v_parts, sink):
    m = s_parts[0].max(axis=-1, keepdims=True)
    for s in s_parts[1:]:
        m = jnp.maximum(m, s.max(axis=-1, keepdims=True))
    if sink is not None:
        m = jnp.maximum(m, sink)
    den = None
    acc = None
    for s, v in zip(s_parts, v_parts):
        e = jnp.exp(s - m)
        d = jnp.sum(e, axis=-1, keepdims=True)
        o = _dot(e.astype(BF16), v)
        den = d if den is None else den + d
        acc = o if acc is None else acc + o
    if sink is not None:
        den = den + jnp.exp(sink - m)
    return acc / den


def _merge_heads(outs, o_ref):
    lane = lax.broadcasted_iota(jnp.int32, outs[0].shape, 1)
    lo = lane < HEAD_DIM
    o_ref[:, 0:LANES] = jnp.where(lo, outs[0], outs[1]).astype(BF16)
    o_ref[:, LANES:2 * LANES] = jnp.where(lo, outs[2], outs[3]).astype(BF16)


def _dense_attn_kernel(*refs, kv_of, v_of, has_sink, with_ctx):
    if has_sink:
        sink_ref, q_ref, kt_ref, v_ref, o_ref = refs
    else:
        q_ref, kt_ref, v_ref, o_ref = refs
        sink_ref = None
    i = pl.program_id(1)

    def run(k_lo, k_hi):
        outs = []
        for h in range(4):
            qh = q_ref[:, h * LANES:(h + 1) * LANES]
            s = _dot(qh, kt_ref[kv_of[h], :, k_lo:k_hi])
            sink = sink_ref[h] if has_sink else None
            outs.append(_softmax_pv([s], [v_ref[v_of[h], k_lo:k_hi, :]], sink))
        _merge_heads(outs, o_ref)

    if with_ctx:
        @pl.when(i < N_LAT_TILES)
        def _():
            run(0, TT)

        @pl.when(i >= N_LAT_TILES)
        def _():
            run(SEQ, TT)
    else:
        run(0, TT)


def _dense_attn_call(q, kt, v, sink, kv_of, v_of, with_ctx, name):
    b = q.shape[0]
    n_kv = kt.shape[1]
    n_tiles = N_TILES if with_ctx else N_LAT_TILES
    has_sink = sink is not None
    in_specs = [
        pl.BlockSpec((None, TOK_TILE, 4 * LANES), lambda bi, i: (bi, i, 0)),
        pl.BlockSpec((None, n_kv, LANES, TT), lambda bi, i: (bi, 0, 0, 0)),
        pl.BlockSpec((None, 2, TT, LANES), lambda bi, i: (bi, 0, 0, 0)),
    ]
    args = [q, kt, v]
    if has_sink:
        in_specs = [pl.BlockSpec(memory_space=pltpu.SMEM)] + in_specs
        args = [sink] + args
    return pl.pallas_call(
        functools.partial(_dense_attn_kernel, kv_of=kv_of, v_of=v_of, has_sink=has_sink, with_ctx=with_ctx),
        grid=(b, n_tiles),
        in_specs=in_specs,
        out_specs=pl.BlockSpec((None, TOK_TILE, GROUP_W), lambda bi, i: (bi, i, 0)),
        out_shape=jax.ShapeDtypeStruct((b, TT, GROUP_W), BF16),
        compiler_params=_params(48),
        name=name,
    )(*args)


WIN_TILE = 128
WIN_BAND = 3 * WIN_TILE


def _win_attn_kernel(sink_ref, q_ref, kt_ref, v_ref, o_ref, *, with_ctx):
    i = pl.program_id(1)
    n_lat = SEQ // WIN_TILE

    def run_band():
        start = jnp.clip((i - 1) * WIN_TILE, 0, SEQ - WIN_BAND)
        start = pl.multiple_of(start, WIN_TILE)
        qpos = i * WIN_TILE + lax.broadcasted_iota(jnp.int32, (WIN_TILE, WIN_BAND), 0)
        kpos = start + lax.broadcasted_iota(jnp.int32, (WIN_TILE, WIN_BAND), 1)
        mask = jnp.abs(kpos - qpos) <= WINDOW
        outs = []
        for h in range(4):
            g = h // 2
            vi = (0, 1, 1, 0)[h]
            qh = q_ref[:, h * LANES:(h + 1) * LANES]
            s_band = _dot(qh, kt_ref[g, :, pl.ds(start, WIN_BAND)])
            s_band = jnp.where(mask, s_band, -jnp.inf)
            s_ctx = _dot(qh, kt_ref[g, :, SEQ:TT])
            outs.append(_softmax_pv(
                [s_band, s_ctx],
                [v_ref[vi, pl.ds(start, WIN_BAND), :], v_ref[vi, SEQ:TT, :]],
                sink_ref[h]))
        _merge_heads(outs, o_ref)

    def run_ctx():
        outs = []
        for h in range(4):
            g = h // 2
            vi = (0, 1, 1, 0)[h]
            qh = q_ref[:, h * LANES:(h + 1) * LANES]
            s_ctx = _dot(qh, kt_ref[g, :, SEQ:TT])
            outs.append(_softmax_pv([s_ctx], [v_ref[vi, SEQ:TT, :]], sink_ref[h]))
        _merge_heads(outs, o_ref)

    if with_ctx:
        pl.when(i < n_lat)(run_band)
        pl.when(i >= n_lat)(run_ctx)
    else:
        run_band()


def _win_attn_call(q, kt, v, sink, with_ctx):
    b = q.shape[0]
    n_tiles = (TT if with_ctx else SEQ) // WIN_TILE
    return pl.pallas_call(
        functools.partial(_win_attn_kernel, with_ctx=with_ctx),
        grid=(b, n_tiles),
        in_specs=[
            pl.BlockSpec(memory_space=pltpu.SMEM),
            pl.BlockSpec((None, WIN_TILE, 4 * LANES), lambda bi, i: (bi, i, 0)),
            pl.BlockSpec((None, 2, LANES, TT), lambda bi, i: (bi, 0, 0, 0)),
            pl.BlockSpec((None, 2, TT, LANES), lambda bi, i: (bi, 0, 0, 0)),
        ],
        out_specs=pl.BlockSpec((None, WIN_TILE, GROUP_W), lambda bi, i: (bi, i, 0)),
        out_shape=jax.ShapeDtypeStruct((b, TT, GROUP_W), BF16),
        compiler_params=_params(32),
        name="attn_win",
    )(sink, q, kt, v)


def _ssd_kernel(pb_ref, dt_ref, cw_ref, cb_ref, dtb_ref, alog_ref, dsk_ref, nw_ref, y_ref,
                xbc_s, yf_s, yb_s, h_s):
    lc = SSD_CHUNK
    n_chunks = TT // lc
    n_lat = SEQ // lc
    n_ctx = CTX_LEN // lc
    row_i = lax.broadcasted_iota(jnp.int32, (lc, lc), 0)
    col_i = lax.broadcasted_iota(jnp.int32, (lc, lc), 1)
    lo = col_i < HEAD_DIM
    tril = row_i >= col_i
    triu = row_i <= col_i
    tri_f = jnp.where(tril, 1.0, 0.0).astype(BF16)
    tri_b = jnp.where(triu, 1.0, 0.0).astype(BF16)

    def conv_body(c, carry):
        r0 = pl.multiple_of(c * lc, lc)
        first = jnp.logical_or(c == 0, c == n_lat)
        last = jnp.logical_or(c == n_lat - 1, c == n_chunks - 1)
        u = pb_ref[pl.ds(r0, lc), GROUP_W:GROUP_W + SSM_CONV_CH].astype(F32)
        rp = pl.multiple_of(jnp.maximum(r0 - BF16_ROWS, 0), BF16_ROWS)
        rn = pl.multiple_of(jnp.minimum(r0 + lc, TT - BF16_ROWS), BF16_ROWS)
        prev_blk = pb_ref[pl.ds(rp, BF16_ROWS), GROUP_W:GROUP_W + SSM_CONV_CH].astype(F32)
        next_blk = pb_ref[pl.ds(rn, BF16_ROWS), GROUP_W:GROUP_W + SSM_CONV_CH].astype(F32)
        prev_row = jnp.where(first, 0.0, prev_blk[BF16_ROWS - 1:BF16_ROWS, :])
        next_row = jnp.where(last, 0.0, next_blk[0:1, :])
        rows = lax.broadcasted_iota(jnp.int32, (lc, SSM_CONV_CH), 0)
        up = jnp.where(rows == 0, prev_row, pltpu.roll(u, 1, 0))
        un = jnp.where(rows == lc - 1, next_row, pltpu.roll(u, lc - 1, 0))
        v = cw_ref[0:1, :] * up + cw_ref[1:2, :] * u + cw_ref[2:3, :] * un + cb_ref[...]
        xbc_s[pl.ds(r0, lc), :] = v * _sigmoid(v)
        return carry

    lax.fori_loop(0, n_chunks, conv_body, 0)

    a_row = -jnp.exp(alog_ref[...])
    dtb = dtb_ref[...]

    def scan_chunk(r0, direction, y_out):
        xs = xbc_s[pl.ds(r0, lc), 0:GROUP_W]
        bm = xbc_s[pl.ds(r0, lc), GROUP_W:2 * GROUP_W]
        cm = xbc_s[pl.ds(r0, lc), 2 * GROUP_W:3 * GROUP_W]
        pre = dt_ref[pl.ds(r0, lc), :] + dtb
        dt = jnp.maximum(pre, 0.0) + jnp.log1p(jnp.exp(-jnp.abs(pre)))
        da = dt * a_row
        t1, t2, t3 = _split3(da)
        tri = tri_f if direction == 0 else tri_b
        cs = _dot(tri, t1) + _dot(tri, t2) + _dot(tri, t3)
        cs_t = cs.T
        tot = cs[lc - 1:lc, :] if direction == 0 else cs[0:1, :]
        keep = tril if direction == 0 else triu
        for g in range(2):
            j0 = direction * 4 + 2 * g
            j1 = j0 + 1
            bm_g = bm[:, g * LANES:(g + 1) * LANES].astype(BF16)
            cm_g = cm[:, g * LANES:(g + 1) * LANES].astype(BF16)
            cb = _dot_nt(cm_g, bm_g)
            xdt = xs[:, g * LANES:(g + 1) * LANES] * jnp.where(lo, dt[:, j0:j0 + 1], dt[:, j1:j1 + 1])
            xdt_b = xdt.astype(BF16)
            y_h = []
            for j in (j0, j1):
                seg = cs[:, j:j + 1] - cs_t[j:j + 1, :]
                decay = jnp.exp(jnp.where(keep, seg, -jnp.inf))
                y_h.append(_dot((cb * decay).astype(BF16), xdt_b))
            y_diag = jnp.where(lo, y_h[0], y_h[1])
            w_state = jnp.where(lo, jnp.exp(tot[:, j0:j0 + 1] - cs[:, j0:j0 + 1]),
                                jnp.exp(tot[:, j1:j1 + 1] - cs[:, j1:j1 + 1]))
            states = _dot((xdt * w_state).T.astype(BF16), bm_g)
            h_in = h_s[direction * 2 + g]
            y_off = _dot_nt(cm_g, h_in.astype(BF16)) * jnp.where(
                lo, jnp.exp(cs[:, j0:j0 + 1]), jnp.exp(cs[:, j1:j1 + 1]))
            y_out[pl.ds(r0, lc), g * LANES:(g + 1) * LANES] = y_diag + y_off
            chunk_decay = jnp.where(row_i[:, 0:1] < HEAD_DIM, jnp.exp(tot[:, j0:j0 + 1]), jnp.exp(tot[:, j1:j1 + 1]))
            h_s[direction * 2 + g] = h_in * chunk_decay + states

    h_s[...] = jnp.zeros_like(h_s)

    def scan_segment(base_chunk, n_seg):
        def body(t, carry):
            rf = pl.multiple_of((base_chunk + t) * lc, lc)
            rb = pl.multiple_of((base_chunk + n_seg - 1 - t) * lc, lc)
            scan_chunk(rf, 0, yf_s)
            scan_chunk(rb, 1, yb_s)
            return carry
        lax.fori_loop(0, n_seg, body, 0)

    scan_segment(n_lat, n_ctx)
    scan_segment(0, n_lat)

    def fin_body(c, carry):
        r0 = pl.multiple_of(c * lc, lc)
        xs = xbc_s[pl.ds(r0, lc), 0:GROUP_W]
        y = yf_s[pl.ds(r0, lc), :] + yb_s[pl.ds(r0, lc), :] + xs * dsk_ref[...]
        z = pb_ref[pl.ds(r0, lc), 0:GROUP_W].astype(F32)
        gte = y * (z * _sigmoid(z))
        ms = jnp.mean(gte * gte, axis=-1, keepdims=True)
        y_ref[pl.ds(r0, lc), :] = (gte * lax.rsqrt(ms + NORM_EPS) * nw_ref[...]).astype(BF16)
        return carry

    lax.fori_loop(0, n_chunks, fin_body, 0)


def _ssd_call(pb, pdt, cw, cb, dtb, alog, dsk, nw):
    b = pb.shape[0]
    return pl.pallas_call(
        _ssd_kernel,
        grid=(b,),
        in_specs=[
            pl.BlockSpec((None, TT, PB_W), lambda bi: (bi, 0, 0)),
            pl.BlockSpec((None, TT, PDT_W), lambda bi: (bi, 0, 0)),
            _resident((3, SSM_CONV_CH)),
            _resident((1, SSM_CONV_CH)),
            _resident((1, LANES)),
            _resident((1, LANES)),
            _resident((1, GROUP_W)),
            _resident((1, GROUP_W)),
        ],
        out_specs=pl.BlockSpec((None, TT, GROUP_W), lambda bi: (bi, 0, 0)),
        out_shape=jax.ShapeDtypeStruct((b, TT, GROUP_W), BF16),
        scratch_shapes=[
            pltpu.VMEM((TT, SSM_CONV_CH), F32),
            pltpu.VMEM((TT, GROUP_W), F32),
            pltpu.VMEM((TT, GROUP_W), F32),
            pltpu.VMEM((4, LANES, SSM_STATE), F32),
        ],
        compiler_params=_params(48, 1),
        name="ssd",
    )(pb, pdt, cw, cb, dtb, alog, dsk, nw)


def _out_kernel(ya_ref, yb_ref, yw_ref, ym_ref, x_ref, m_ref, w_ref, o_ref):
    acc = _dot(ya_ref[...], w_ref[0 * GROUP_W:1 * GROUP_W, :])
    acc += _dot(yb_ref[...], w_ref[1 * GROUP_W:2 * GROUP_W, :])
    acc += _dot(yw_ref[...], w_ref[2 * GROUP_W:3 * GROUP_W, :])
    acc += _dot(ym_ref[...], w_ref[3 * GROUP_W:4 * GROUP_W, :])
    o_ref[...] = x_ref[...] + m_ref[2:3, :] * acc


def _out_call(ys, x, mods, w, is_ctx):
    b, t, _ = x.shape
    n_tiles = t // TOK_TILE
    off = N_LAT_TILES if is_ctx else 0
    mi = 1 if is_ctx else 0
    y_spec = pl.BlockSpec((None, TOK_TILE, GROUP_W), lambda bi, i: (bi, i + off, 0))
    return pl.pallas_call(
        _out_kernel,
        grid=(b, n_tiles),
        in_specs=[y_spec, y_spec, y_spec, y_spec,
                  pl.BlockSpec((None, TOK_TILE, D_MODEL), lambda bi, i: (bi, i, 0)),
                  pl.BlockSpec((None, None, 8, D_MODEL), lambda bi, i: (bi, mi, 0, 0)),
                  _resident((4 * GROUP_W, D_MODEL))],
        out_specs=pl.BlockSpec((None, TOK_TILE, D_MODEL), lambda bi, i: (bi, i, 0)),
        out_shape=jax.ShapeDtypeStruct((b, t, D_MODEL), F32),
        compiler_params=_params(32),
        name="out_ctx" if is_ctx else "out_lat",
    )(*ys, x, mods, w)


def _ffn_kernel(xp_ref, x_ref, xn_ref, nw_ref, m_ref, wa_ref, wg_ref, cw_ref, cb_ref, wd_ref, fw_ref, o_ref,
                acc_s, *, tm, final_norm):
    i = pl.program_id(1)
    n_i = pl.num_programs(1)
    x = x_ref[...]
    xa = jnp.concatenate([x, xp_ref[...], xn_ref[...]], axis=0)
    ms = jnp.mean(xa * xa, axis=-1, keepdims=True)
    y = xa * lax.rsqrt(ms + NORM_EPS) * nw_ref[...]
    h = (y * (1.0 + m_ref[4:5, :]) + m_ref[3:4, :]).astype(BF16)
    h_mid = h[0:tm, :]
    rid = lax.broadcasted_iota(jnp.int32, (tm, FFN_CHUNK), 0)
    has_prev = i > 0
    has_next = i < n_i - 1
    acc_s[...] = jnp.zeros_like(acc_s)

    def body(j, carry):
        c0 = pl.multiple_of(j * FFN_CHUNK, FFN_CHUNK)
        a = _dot(h_mid, wa_ref[:, pl.ds(c0, FFN_CHUNK)])
        g = _dot(h, wg_ref[:, pl.ds(c0, FFN_CHUNK)])
        g_mid = g[0:tm, :]
        row_before = jnp.where(has_prev, g[tm + FFN_HALO - 1:tm + FFN_HALO, :], 0.0)
        row_after = jnp.where(has_next, g[tm + FFN_HALO:tm + FFN_HALO + 1, :], 0.0)
        g_prev = jnp.where(rid == 0, row_before, pltpu.roll(g_mid, 1, 0))
        g_next = jnp.where(rid == tm - 1, row_after, pltpu.roll(g_mid, tm - 1, 0))
        cw = cw_ref[:, pl.ds(c0, FFN_CHUNK)]
        gc = cw[0:1, :] * g_prev + cw[1:2, :] * g_mid + cw[2:3, :] * g_next + cb_ref[:, pl.ds(c0, FFN_CHUNK)]
        u = (a * (gc * _sigmoid(gc))).astype(BF16)
        acc_s[...] += _dot(u, wd_ref[pl.ds(c0, FFN_CHUNK), :])
        return carry

    lax.fori_loop(0, D_FF // FFN_CHUNK, body, 0)
    out = x + m_ref[5:6, :] * acc_s[...]
    if final_norm:
        ms2 = jnp.mean(out * out, axis=-1, keepdims=True)
        out = out * lax.rsqrt(ms2 + NORM_EPS) * fw_ref[...]
    o_ref[...] = out


def _ffn_call(x, norm_w, mods, wa, wg, cw, cb, wd, fw, tm, is_ctx, final_norm):
    b, t, _ = x.shape
    n_tiles = t // tm
    hb = tm // FFN_HALO
    n_hblk = t // FFN_HALO
    mi = 1 if is_ctx else 0
    return pl.pallas_call(
        functools.partial(_ffn_kernel, tm=tm, final_norm=final_norm),
        grid=(b, n_tiles),
        in_specs=[
            pl.BlockSpec((None, FFN_HALO, D_MODEL), lambda bi, i: (bi, jnp.maximum(i * hb - 1, 0), 0)),
            pl.BlockSpec((None, tm, D_MODEL), lambda bi, i: (bi, i, 0)),
            pl.BlockSpec((None, FFN_HALO, D_MODEL), lambda bi, i: (bi, jnp.minimum((i + 1) * hb, n_hblk - 1), 0)),
            _resident((1, D_MODEL)),
            pl.BlockSpec((None, None, 8, D_MODEL), lambda bi, i: (bi, mi, 0, 0)),
            _resident((D_MODEL, D_FF)),
            _resident((D_MODEL, D_FF)),
            _resident((3, D_FF)),
            _resident((1, D_FF)),
            _resident((D_FF, D_MODEL)),
            _resident((1, D_MODEL)),
        ],
        out_specs=pl.BlockSpec((None, tm, D_MODEL), lambda bi, i: (bi, i, 0)),
        out_shape=jax.ShapeDtypeStruct((b, t, D_MODEL), F32),
        scratch_shapes=[pltpu.VMEM((tm, D_MODEL), F32)],
        compiler_params=_params(56),
        name="ffn_ctx" if is_ctx else "ffn_lat",
    )(x, x, x, norm_w, mods, wa, wg, cw, cb, wd, fw)


def _pad_cols(w, width):
    return jnp.pad(w, ((0, 0), (0, width - w.shape[1])))


def _proj_weight(w_in_l):
    a_cols = 512
    b_cols = GROUP_W + SSM_CONV_CH + 8
    c_cols = 512
    wa = w_in_l[:, :a_cols]
    wb = w_in_l[:, a_cols:a_cols + b_cols]
    wc = w_in_l[:, a_cols + b_cols:a_cols + b_cols + c_cols]
    wd = w_in_l[:, a_cols + b_cols + c_cols:]
    w_zx = wb[:, :GROUP_W + SSM_CONV_CH]
    w_dt = _pad_cols(wb[:, GROUP_W + SSM_CONV_CH:], PDT_W)
    w_cq = _pad_cols(wd[:, :MLA_Q_RANK], 2 * LANES)
    w_ckv = wd[:, MLA_Q_RANK:MLA_Q_RANK + MLA_KV_RANK]
    w_kr = wd[:, MLA_Q_RANK + MLA_KV_RANK:]
    zeros = lambda n: jnp.zeros((D_MODEL, n), w_in_l.dtype)
    w_kr = jnp.concatenate([zeros(MLA_NOPE), w_kr, zeros(LANES - MLA_NOPE - MLA_ROPE)], axis=1)
    return jnp.concatenate([wa, w_zx, w_dt, wc, w_cq, w_ckv, w_kr], axis=1).astype(BF16)


def _mla_weights(w_uq_l, w_ukv_l):
    dq = MLA_NOPE + MLA_ROPE
    wq = w_uq_l.reshape(MLA_Q_RANK, 4, dq)
    wq = jnp.pad(wq, ((0, 2 * LANES - MLA_Q_RANK), (0, 0), (0, LANES - dq))).reshape(2 * LANES, 4 * LANES)
    wkv = w_ukv_l.reshape(MLA_KV_RANK, 4, MLA_NOPE + HEAD_DIM)
    wk = jnp.pad(wkv[:, :, :MLA_NOPE], ((0, 0), (0, 0), (0, LANES - MLA_NOPE))).reshape(MLA_KV_RANK, 4 * LANES)
    wv = wkv[:, :, MLA_NOPE:].reshape(MLA_KV_RANK, 4 * HEAD_DIM)
    return wq.astype(BF16), jnp.concatenate([wk, wv], axis=1).astype(BF16)


def kernel(x, c, ctx, c_ctx, norm1_w, w_mod, b_mod, w_in, attn_q_norm, attn_k_norm, ssm_conv_w, ssm_conv_b,
           ssm_dt_bias, ssm_a_log, ssm_d, ssm_norm_w, win_sink, mla_q_norm, mla_w_uq, mla_kv_norm, mla_w_ukv,
           w_out, norm2_w, ffn_w_up, ffn_conv_w, ffn_conv_b, ffn_w_down, final_norm_w):
    bsz = x.shape[0]
    depth = w_in.shape[0]
    assert x.shape[1:] == (SEQ, D_MODEL) and ctx.shape[1:] == (CTX_LEN, D_MODEL)

    cos_a, sin_a = _axial_tables(HEAD_DIM, 0, 2)
    cos_m, sin_m = _axial_tables(MLA_ROPE, MLA_NOPE, 1)

    n_rows = ((bsz + 1 + 7) // 8) * 8
    cvec = jnp.concatenate([c, c_ctx[None, :], jnp.zeros((n_rows - bsz - 1, D_MODEL), F32)], axis=0)
    mods_all = _mod_call(cvec, w_mod, b_mod.reshape(depth, 1, 6 * D_MODEL))

    x_lat, x_ctx = x, ctx
    for l in range(depth):
        with_ctx = l < depth - 1
        m_lat = mods_all[l, :bsz].reshape(bsz, 6, D_MODEL)
        m_ctx = jnp.broadcast_to(mods_all[l, bsz].reshape(1, 6, D_MODEL), (bsz, 6, D_MODEL))
        mods = jnp.pad(jnp.stack([m_lat, m_ctx], axis=1), ((0, 0), (0, 0), (0, 2), (0, 0)))

        pa, pb, pdt, pc, pd = _proj_call(x_lat, x_ctx, norm1_w[l].reshape(1, D_MODEL), mods, _proj_weight(w_in[l]))

        qn = jnp.tile(attn_q_norm[l], 2).reshape(1, LANES)
        kn = jnp.tile(attn_k_norm[l], 2).reshape(1, LANES)
        qa, kta, va = _prep_gqa_call(pa, cos_a, sin_a, qn, kn, True, "prep_gqa")
        ya = _dense_attn_call(qa, kta, va, None, (0, 0, 1, 1), (0, 1, 1, 0), with_ctx, "attn_gqa")

        yb = _ssd_call(
            pb, pdt, ssm_conv_w[l].T, ssm_conv_b[l].reshape(1, SSM_CONV_CH),
            _pad_cols(ssm_dt_bias[l].reshape(1, 8), LANES), _pad_cols(ssm_a_log[l].reshape(1, 8), LANES),
            jnp.repeat(ssm_d[l], HEAD_DIM).reshape(1, GROUP_W), ssm_norm_w[l].reshape(1, GROUP_W))

        qc, ktc, vc = _prep_gqa_call(pc, cos_a, sin_a, qn, kn, False, "prep_win")
        yw = _win_attn_call(qc, ktc, vc, win_sink[l], with_ctx)

        wq, wkv = _mla_weights(mla_w_uq[l], mla_w_ukv[l])
        qd, ktd, vd = _prep_mla_call(pd, cos_m, sin_m, _pad_cols(mla_q_norm[l].reshape(1, MLA_Q_RANK), 2 * LANES),
                                     mla_kv_norm[l].reshape(1, MLA_KV_RANK), wq, wkv)
        ym = _dense_attn_call(qd, ktd, vd, None, (0, 1, 2, 3), (0, 0, 1, 1), with_ctx, "attn_mla")

        ys = (ya, yb, yw, ym)
        w_o = w_out[l].astype(BF16)
        wa_up = ffn_w_up[l][:, :D_FF].astype(BF16)
        wg_up = ffn_w_up[l][:, D_FF:].astype(BF16)
        cw = ffn_conv_w[l].T
        cb = ffn_conv_b[l].reshape(1, D_FF)
        wd = ffn_w_down[l].astype(BF16)
        n2 = norm2_w[l].reshape(1, D_MODEL)
        fw = final_norm_w.reshape(1, D_MODEL)
        last = l == depth - 1

        x1 = _out_call(ys, x_lat, mods, w_o, False)
        x_lat = _ffn_call(x1, n2, mods, wa_up, wg_up, cw, cb, wd, fw, 512, False, last)
        if with_ctx:
            xc1 = _out_call(ys, x_ctx, mods, w_o, True)
            x_ctx = _ffn_call(xc1, n2, mods, wa_up, wg_up, cw, cb, wd, fw, CTX_LEN, True, False)
    return x_lat
```

```python
import functools

import numpy as np
import jax
import jax.numpy as jnp
from jax import lax
from jax.experimental import pallas as pl
from jax.experimental.pallas import tpu as pltpu

F32 = jnp.float32
BF16 = jnp.bfloat16

D_MODEL = 1024
SEQ = 2048
CTX_LEN = 256
TT = SEQ + CTX_LEN
GRID_W = 64
ROPE_THETA = 10000.0
NORM_EPS = 1e-6
HEAD_DIM = 64
GROUP_W = 256
WINDOW = 128
SSM_STATE = 128
SSM_CONV_CH = 768
MLA_NOPE = 64
MLA_ROPE = 32
MLA_Q_RANK = 192
MLA_KV_RANK = 128
D_FF = 2816

LANES = 128
BF16_ROWS = 16
TOK_TILE = 256
N_LAT_TILES = SEQ // TOK_TILE
N_TILES = TT // TOK_TILE
SSD_CHUNK = 128
FFN_CHUNK = 256
FFN_HALO = 8
LOG2E = 1.4426950408889634

PA_W, PB_W, PDT_W, PC_W, PD_W = 512, 1024, 128, 512, 512
OFF_A = 0
OFF_B = OFF_A + PA_W
OFF_DT = OFF_B + PB_W
OFF_C = OFF_DT + PDT_W
OFF_D = OFF_C + PC_W
PROJ_W = OFF_D + PD_W


def _resident(shape):
    return pl.BlockSpec(shape, lambda *_: (0,) * len(shape), pipeline_mode=pl.Buffered(1))


def _params(vmem_mb, ndims=2):
    return pltpu.CompilerParams(
        dimension_semantics=("arbitrary",) * ndims,
        vmem_limit_bytes=vmem_mb * 1024 * 1024,
    )


def _sigmoid(v):
    return 1.0 / (1.0 + jnp.exp(-v))


def _dot(a, b):
    return jnp.dot(a, b, preferred_element_type=F32)


def _dot_nt(a, b):
    return lax.dot_general(a, b, (((1,), (1,)), ((), ())), preferred_element_type=F32)


def _split3(v):
    t1 = v.astype(BF16)
    r1 = v - t1.astype(F32)
    t2 = r1.astype(BF16)
    t3 = (r1 - t2.astype(F32)).astype(BF16)
    return t1, t2, t3


def _mod_kernel(c_ref, w_ref, b_ref, o_ref):
    a = c_ref[...]
    s = a * _sigmoid(a)
    w = w_ref[...]
    s_hi = s.astype(BF16)
    s_lo = (s - s_hi.astype(F32)).astype(BF16)
    w_hi = w.astype(BF16)
    w_lo = (w - w_hi.astype(F32)).astype(BF16)
    o_ref[...] = _dot(s_hi, w_hi) + _dot(s_hi, w_lo) + _dot(s_lo, w_hi) + b_ref[...]


def _mod_call(cvec, w_mod, b_mod):
    n_layers, d, n6 = w_mod.shape
    rows = cvec.shape[0]
    tn = 1024
    return pl.pallas_call(
        _mod_kernel,
        grid=(n_layers, n6 // tn),
        in_specs=[
            pl.BlockSpec((rows, d), lambda l, j: (0, 0)),
            pl.BlockSpec((None, d, tn), lambda l, j: (l, 0, j)),
            pl.BlockSpec((None, 1, tn), lambda l, j: (l, 0, j)),
        ],
        out_specs=pl.BlockSpec((None, rows, tn), lambda l, j: (l, 0, j)),
        out_shape=jax.ShapeDtypeStruct((n_layers, rows, n6), F32),
        compiler_params=_params(40),
        name="mod",
    )(cvec, w_mod, b_mod)


def _rope(x, cos, sin_signed, half):
    lane = lax.broadcasted_iota(jnp.int32, x.shape, 1)
    first = (lane & (2 * half - 1)) < half
    partner = jnp.where(first, pltpu.roll(x, LANES - half, 1), pltpu.roll(x, half, 1))
    return x * cos + partner * sin_signed


def _axial_tables(rot_dim, lane_lo, reps):
    half = rot_dim // 2
    t = np.arange(SEQ)
    row = (t // GRID_W).astype(np.float32)
    col = (t % GRID_W).astype(np.float32)
    inv_freq = np.power(np.float32(ROPE_THETA), -np.arange(0, half, 2, dtype=np.float32) / np.float32(half))
    inv_freq = inv_freq.astype(np.float32)
    ang_r = (row[:, None] * inv_freq[None, :]).astype(np.float32)
    ang_c = (col[:, None] * inv_freq[None, :]).astype(np.float32)
    ang = np.concatenate([ang_r, ang_r, ang_c, ang_c], axis=-1).astype(np.float64)
    cos = np.cos(ang)
    sin = np.sin(ang)
    quarter = half // 2
    sign = np.where((np.arange(rot_dim) % half) < quarter, -1.0, 1.0)
    cos_t = np.ones((TT, LANES), np.float64)
    sin_t = np.zeros((TT, LANES), np.float64)
    for r in range(reps):
        lo = lane_lo + r * rot_dim
        cos_t[:SEQ, lo:lo + rot_dim] = cos
        sin_t[:SEQ, lo:lo + rot_dim] = sin * sign[None, :]
    return jnp.asarray(cos_t, F32), jnp.asarray(sin_t, F32)


def _store_value_slabs(pair, swapped, lo, v_ref):
    v_ref[0] = jnp.where(lo, pair, 1.0).astype(BF16)
    v_ref[1] = jnp.where(lo, 1.0, swapped).astype(BF16)
    v_ref[2] = jnp.where(lo, swapped, 1.0).astype(BF16)
    v_ref[3] = jnp.where(lo, 1.0, pair).astype(BF16)


def _gqa_prep(p, cos, sin, qn, kn, do_norm, q_ref, k_ref, v_ref):
    lane = lax.broadcasted_iota(jnp.int32, (p.shape[0], LANES), 1)
    lo = lane < HEAD_DIM

    def head_norm(v, gain):
        v2 = v * v
        s_lo = jnp.sum(jnp.where(lo, v2, 0.0), axis=-1, keepdims=True)
        s_hi = jnp.sum(jnp.where(lo, 0.0, v2), axis=-1, keepdims=True)
        ms = jnp.where(lo, s_lo, s_hi) * (1.0 / HEAD_DIM)
        return v * lax.rsqrt(ms + NORM_EPS) * gain

    scale = HEAD_DIM ** -0.5 * LOG2E
    for j in range(2):
        v = p[:, j * LANES:(j + 1) * LANES]
        if do_norm:
            v = head_norm(v, qn)
        v = _rope(v, cos, sin, HEAD_DIM // 4) * scale
        sw = pltpu.roll(v, HEAD_DIM, 1)
        if j == 0:
            q_ref[:, 0:LANES] = jnp.where(lo, v, 0.0).astype(BF16)
            q_ref[:, LANES:2 * LANES] = jnp.where(lo, sw, 0.0).astype(BF16)
        else:
            q_ref[:, 2 * LANES:3 * LANES] = jnp.where(lo, 0.0, sw).astype(BF16)
            q_ref[:, 3 * LANES:4 * LANES] = jnp.where(lo, 0.0, v).astype(BF16)

    k = p[:, 2 * LANES:3 * LANES]
    if do_norm:
        k = head_norm(k, kn)
    k_ref[0] = _rope(k, cos, sin, HEAD_DIM // 4).astype(BF16)

    vv = p[:, 3 * LANES:4 * LANES]
    _store_value_slabs(vv, pltpu.roll(vv, HEAD_DIM, 1), lo, v_ref)


def _mla_prep(p, cos, sin, gq, gkv, wq_ref, wkv_ref, q_ref, k_ref, v_ref):
    half = MLA_ROPE // 4
    cq = p[:, 0:2 * LANES]
    ms = jnp.sum(cq * cq, axis=-1, keepdims=True) * (1.0 / MLA_Q_RANK)
    cqn = (cq * lax.rsqrt(ms + NORM_EPS) * gq).astype(BF16)
    q = _dot(cqn, wq_ref[...])
    scale = (MLA_NOPE + MLA_ROPE) ** -0.5 * LOG2E
    for h in range(4):
        qh = _rope(q[:, h * LANES:(h + 1) * LANES], cos, sin, half) * scale
        q_ref[:, h * LANES:(h + 1) * LANES] = qh.astype(BF16)

    ckv = p[:, 2 * LANES:3 * LANES]
    ms = jnp.mean(ckv * ckv, axis=-1, keepdims=True)
    ckvn = (ckv * lax.rsqrt(ms + NORM_EPS) * gkv).astype(BF16)
    kv = _dot(ckvn, wkv_ref[...])
    kr = _rope(p[:, 3 * LANES:4 * LANES], cos, sin, half)
    for h in range(4):
        k_ref[h] = (kv[:, h * LANES:(h + 1) * LANES] + kr).astype(BF16)
    lane = lax.broadcasted_iota(jnp.int32, (p.shape[0], LANES), 1)
    lo = lane < HEAD_DIM
    v01 = kv[:, 4 * LANES:5 * LANES]
    v23 = kv[:, 5 * LANES:6 * LANES]
    v_ref[0] = jnp.where(lo, v01, 1.0).astype(BF16)
    v_ref[1] = jnp.where(lo, 1.0, v01).astype(BF16)
    v_ref[2] = jnp.where(lo, v23, 1.0).astype(BF16)
    v_ref[3] = jnp.where(lo, 1.0, v23).astype(BF16)


def _proj_kernel(xl_ref, xc_ref, nw_ref, m_ref, w_ref, cosa_ref, sina_ref, cosm_ref, sinm_ref,
                 qn_ref, kn_ref, gq_ref, gkv_ref, wq_ref, wkv_ref,
                 qa_ref, ka_ref, va_ref, pb_ref, pdt_ref, qc_ref, kc_ref, vc_ref, qd_ref, kd_ref, vd_ref):
    i = pl.program_id(1)
    x = jnp.where(i < N_LAT_TILES, xl_ref[...], xc_ref[...])
    ms = jnp.mean(x * x, axis=-1, keepdims=True)
    y = x * lax.rsqrt(ms + NORM_EPS) * nw_ref[...]
    h = (y * (1.0 + m_ref[1:2, :]) + m_ref[0:1, :]).astype(BF16)
    cosa = cosa_ref[...]
    sina = sina_ref[...]
    pb_ref[...] = _dot(h, w_ref[:, OFF_B:OFF_B + PB_W]).astype(BF16)
    pdt_ref[...] = _dot(h, w_ref[:, OFF_DT:OFF_DT + PDT_W])
    _gqa_prep(_dot(h, w_ref[:, OFF_A:OFF_A + PA_W]), cosa, sina, qn_ref[...], kn_ref[...], True,
              qa_ref, ka_ref, va_ref)
    _gqa_prep(_dot(h, w_ref[:, OFF_C:OFF_C + PC_W]), cosa, sina, None, None, False,
              qc_ref, kc_ref, vc_ref)
    _mla_prep(_dot(h, w_ref[:, OFF_D:OFF_D + PD_W]), cosm_ref[...], sinm_ref[...], gq_ref[...], gkv_ref[...],
              wq_ref, wkv_ref, qd_ref, kd_ref, vd_ref)


def _proj_call(x_lat, x_ctx, norm_w, mods, w, tables, qn, kn, gq, gkv, wq, wkv):
    b = x_lat.shape[0]
    tok = lambda width: pl.BlockSpec((None, TOK_TILE, width), lambda bi, i: (bi, i, 0))
    slab = lambda n: pl.BlockSpec((None, n, TOK_TILE, LANES), lambda bi, i: (bi, 0, i, 0))
    table = pl.BlockSpec((TOK_TILE, LANES), lambda bi, i: (i, 0))
    q_shape = jax.ShapeDtypeStruct((b, TT, 4 * LANES), BF16)
    slab_shape = lambda n: jax.ShapeDtypeStruct((b, n, TT, LANES), BF16)
    return pl.pallas_call(
        _proj_kernel,
        grid=(b, N_TILES),
        in_specs=[
            pl.BlockSpec((None, TOK_TILE, D_MODEL), lambda bi, i: (bi, jnp.minimum(i, N_LAT_TILES - 1), 0)),
            pl.BlockSpec((None, TOK_TILE, D_MODEL), lambda bi, i: (bi, 0, 0)),
            _resident((1, D_MODEL)),
            pl.BlockSpec((None, None, 8, D_MODEL), lambda bi, i: (bi, i // N_LAT_TILES, 0, 0)),
            _resident((D_MODEL, PROJ_W)),
            table, table, table, table,
            _resident((1, LANES)), _resident((1, LANES)),
            _resident((1, 2 * LANES)), _resident((1, LANES)),
            _resident((2 * LANES, 4 * LANES)), _resident((LANES, 6 * LANES)),
        ],
        out_specs=[tok(4 * LANES), slab(1), slab(4), tok(PB_W), tok(PDT_W),
                   tok(4 * LANES), slab(1), slab(4), tok(4 * LANES), slab(4), slab(4)],
        out_shape=[q_shape, slab_shape(1), slab_shape(4),
                   jax.ShapeDtypeStruct((b, TT, PB_W), BF16), jax.ShapeDtypeStruct((b, TT, PDT_W), F32),
                   q_shape, slab_shape(1), slab_shape(4), q_shape, slab_shape(4), slab_shape(4)],
        compiler_params=_params(48),
        name="proj",
    )(x_lat, x_ctx, norm_w, mods, w, *tables, qn, kn, gq, gkv, wq, wkv)


def _softmax_pv(s_parts, v_parts, sink):
    m = s_parts[0].max(axis=-1, keepdims=True)
    for s in s_parts[1:]:
        m = jnp.maximum(m, s.max(axis=-1, keepdims=True))
    if sink is not None:
        m = jnp.maximum(m, sink)
    acc = None
    for s, v in zip(s_parts, v_parts):
        o = _dot(jnp.exp2(s - m).astype(BF16), v)
        acc = o if acc is None else acc + o
    den = pltpu.roll(acc, HEAD_DIM, 1)
    if sink is not None:
        den = den + jnp.exp2(sink - m)
    return acc / den


def _merge_heads(outs, o_ref):
    lane = lax.broadcasted_iota(jnp.int32, outs[0].shape, 1)
    lo = lane < HEAD_DIM
    o_ref[:, 0:LANES] = jnp.where(lo, outs[0], outs[1]).astype(BF16)
    o_ref[:, LANES:2 * LANES] = jnp.where(lo, outs[2], outs[3]).astype(BF16)


def _dense_attn_kernel(*refs, k_of, has_sink, with_ctx):
    if has_sink:
        sink_ref, q_ref, k_ref, v_ref, o_ref = refs
    else:
        q_ref, k_ref, v_ref, o_ref = refs
        sink_ref = None
    i = pl.program_id(1)

    def run(k_lo, k_hi):
        outs = []
        for h in range(4):
            qh = q_ref[:, h * LANES:(h + 1) * LANES]
            s = _dot_nt(qh, k_ref[k_of[h], k_lo:k_hi, :])
            sink = sink_ref[h] * LOG2E if has_sink else None
            outs.append(_softmax_pv([s], [v_ref[h, k_lo:k_hi, :]], sink))
        _merge_heads(outs, o_ref)

    if with_ctx:
        @pl.when(i < N_LAT_TILES)
        def _():
            run(0, TT)

        @pl.when(i >= N_LAT_TILES)
        def _():
            run(SEQ, TT)
    else:
        run(0, TT)


def _dense_attn_call(q, k, v, sink, k_of, with_ctx, name):
    b = q.shape[0]
    n_k = k.shape[1]
    n_tiles = N_TILES if with_ctx else N_LAT_TILES
    has_sink = sink is not None
    in_specs = [
        pl.BlockSpec((None, TOK_TILE, 4 * LANES), lambda bi, i: (bi, i, 0)),
        pl.BlockSpec((None, n_k, TT, LANES), lambda bi, i: (bi, 0, 0, 0)),
        pl.BlockSpec((None, 4, TT, LANES), lambda bi, i: (bi, 0, 0, 0)),
    ]
    args = [q, k, v]
    if has_sink:
        in_specs = [pl.BlockSpec(memory_space=pltpu.SMEM)] + in_specs
        args = [sink] + args
    return pl.pallas_call(
        functools.partial(_dense_attn_kernel, k_of=k_of, has_sink=has_sink, with_ctx=with_ctx),
        grid=(b, n_tiles),
        in_specs=in_specs,
        out_specs=pl.BlockSpec((None, TOK_TILE, GROUP_W), lambda bi, i: (bi, i, 0)),
        out_shape=jax.ShapeDtypeStruct((b, TT, GROUP_W), BF16),
        compiler_params=_params(48),
        name=name,
    )(*args)


WIN_TILE = 256
WIN_BAND = WIN_TILE + 2 * WINDOW


def _win_attn_kernel(sink_ref, q_ref, k_ref, v_ref, o_ref, *, with_ctx):
    i = pl.program_id(1)
    n_lat = SEQ // WIN_TILE

    def run_band():
        start = jnp.clip(i * WIN_TILE - WINDOW, 0, SEQ - WIN_BAND)
        start = pl.multiple_of(start, WINDOW)
        qpos = i * WIN_TILE + lax.broadcasted_iota(jnp.int32, (WIN_TILE, WIN_BAND), 0)
        kpos = start + lax.broadcasted_iota(jnp.int32, (WIN_TILE, WIN_BAND), 1)
        mask = jnp.abs(kpos - qpos) <= WINDOW
        k_band = k_ref[0, pl.ds(start, WIN_BAND), :]
        k_ctx = k_ref[0, SEQ:TT, :]
        outs = []
        for h in range(4):
            qh = q_ref[:, h * LANES:(h + 1) * LANES]
            s_band = jnp.where(mask, _dot_nt(qh, k_band), -jnp.inf)
            s_ctx = _dot_nt(qh, k_ctx)
            outs.append(_softmax_pv(
                [s_band, s_ctx],
                [v_ref[h, pl.ds(start, WIN_BAND), :], v_ref[h, SEQ:TT, :]],
                sink_ref[h] * LOG2E))
        _merge_heads(outs, o_ref)

    def run_ctx():
        k_ctx = k_ref[0, SEQ:TT, :]
        outs = []
        for h in range(4):
            qh = q_ref[:, h * LANES:(h + 1) * LANES]
            outs.append(_softmax_pv([_dot_nt(qh, k_ctx)], [v_ref[h, SEQ:TT, :]], sink_ref[h] * LOG2E))
        _merge_heads(outs, o_ref)

    if with_ctx:
        pl.when(i < n_lat)(run_band)
        pl.when(i >= n_lat)(run_ctx)
    else:
        run_band()


def _win_attn_call(q, k, v, sink, with_ctx):
    b = q.shape[0]
    n_tiles = (TT if with_ctx else SEQ) // WIN_TILE
    return pl.pallas_call(
        functools.partial(_win_attn_kernel, with_ctx=with_ctx),
        grid=(b, n_tiles),
        in_specs=[
            pl.BlockSpec(memory_space=pltpu.SMEM),
            pl.BlockSpec((None, WIN_TILE, 4 * LANES), lambda bi, i: (bi, i, 0)),
            pl.BlockSpec((None, 1, TT, LANES), lambda bi, i: (bi, 0, 0, 0)),
            pl.BlockSpec((None, 4, TT, LANES), lambda bi, i: (bi, 0, 0, 0)),
        ],
        out_specs=pl.BlockSpec((None, WIN_TILE, GROUP_W), lambda bi, i: (bi, i, 0)),
        out_shape=jax.ShapeDtypeStruct((b, TT, GROUP_W), BF16),
        compiler_params=_params(32),
        name="attn_win",
    )(sink, q, k, v)


def _ssd_kernel(pb_ref, dt_ref, cw_ref, cb_ref, dtb_ref, alog_ref, dsk_ref, nw_ref, y_ref,
                xbc_s, yf_s, yb_s, h_s):
    lc = SSD_CHUNK
    n_chunks = TT // lc
    n_lat = SEQ // lc
    n_ctx = CTX_LEN // lc
    row_i = lax.broadcasted_iota(jnp.int32, (lc, lc), 0)
    col_i = lax.broadcasted_iota(jnp.int32, (lc, lc), 1)
    lo = col_i < HEAD_DIM
    tril = row_i >= col_i
    triu = row_i <= col_i
    tri_f = jnp.where(tril, 1.0, 0.0).astype(BF16)
    tri_b = jnp.where(triu, 1.0, 0.0).astype(BF16)

    def conv_body(c, carry):
        r0 = pl.multiple_of(c * lc, lc)
        first = jnp.logical_or(c == 0, c == n_lat)
        last = jnp.logical_or(c == n_lat - 1, c == n_chunks - 1)
        u = pb_ref[pl.ds(r0, lc), GROUP_W:GROUP_W + SSM_CONV_CH].astype(F32)
        rp = pl.multiple_of(jnp.maximum(r0 - BF16_ROWS, 0), BF16_ROWS)
        rn = pl.multiple_of(jnp.minimum(r0 + lc, TT - BF16_ROWS), BF16_ROWS)
        prev_blk = pb_ref[pl.ds(rp, BF16_ROWS), GROUP_W:GROUP_W + SSM_CONV_CH].astype(F32)
        next_blk = pb_ref[pl.ds(rn, BF16_ROWS), GROUP_W:GROUP_W + SSM_CONV_CH].astype(F32)
        prev_row = jnp.where(first, 0.0, prev_blk[BF16_ROWS - 1:BF16_ROWS, :])
        next_row = jnp.where(last, 0.0, next_blk[0:1, :])
        rows = lax.broadcasted_iota(jnp.int32, (lc, SSM_CONV_CH), 0)
        up = jnp.where(rows == 0, prev_row, pltpu.roll(u, 1, 0))
        un = jnp.where(rows == lc - 1, next_row, pltpu.roll(u, lc - 1, 0))
        v = cw_ref[0:1, :] * up + cw_ref[1:2, :] * u + cw_ref[2:3, :] * un + cb_ref[...]
        xbc_s[pl.ds(r0, lc), :] = v * _sigmoid(v)
        return carry

    lax.fori_loop(0, n_chunks, conv_body, 0)

    a_row = -jnp.exp(alog_ref[...])
    dtb = dtb_ref[...]

    def scan_chunk(r0, direction, y_out):
        xs = xbc_s[pl.ds(r0, lc), 0:GROUP_W]
        bm = xbc_s[pl.ds(r0, lc), GROUP_W:2 * GROUP_W]
        cm = xbc_s[pl.ds(r0, lc), 2 * GROUP_W:3 * GROUP_W]
        pre = dt_ref[pl.ds(r0, lc), :] + dtb
        dt = jnp.maximum(pre, 0.0) + jnp.log1p(jnp.exp(-jnp.abs(pre)))
        da = dt * a_row
        t1, t2, t3 = _split3(da)
        tri = tri_f if direction == 0 else tri_b
        cs = _dot(tri, t1) + _dot(tri, t2) + _dot(tri, t3)
        cs_t = cs.T
        tot = cs[lc - 1:lc, :] if direction == 0 else cs[0:1, :]
        keep = tril if direction == 0 else triu
        for g in range(2):
            j0 = direction * 4 + 2 * g
            j1 = j0 + 1
            bm_g = bm[:, g * LANES:(g + 1) * LANES].astype(BF16)
            cm_g = cm[:, g * LANES:(g + 1) * LANES].astype(BF16)
            cb = _dot_nt(cm_g, bm_g)
            xdt = xs[:, g * LANES:(g + 1) * LANES] * jnp.where(lo, dt[:, j0:j0 + 1], dt[:, j1:j1 + 1])
            xdt_b = xdt.astype(BF16)
            y_h = []
            for j in (j0, j1):
                seg = cs[:, j:j + 1] - cs_t[j:j + 1, :]
                decay = jnp.exp(jnp.where(keep, seg, -jnp.inf))
                y_h.append(_dot((cb * decay).astype(BF16), xdt_b))
            y_diag = jnp.where(lo, y_h[0], y_h[1])
            w_state = jnp.where(lo, jnp.exp(tot[:, j0:j0 + 1] - cs[:, j0:j0 + 1]),
                                jnp.exp(tot[:, j1:j1 + 1] - cs[:, j1:j1 + 1]))
            states = _dot((xdt * w_state).T.astype(BF16), bm_g)
            h_in = h_s[direction * 2 + g]
            y_off = _dot_nt(cm_g, h_in.astype(BF16)) * jnp.where(
                lo, jnp.exp(cs[:, j0:j0 + 1]), jnp.exp(cs[:, j1:j1 + 1]))
            y_out[pl.ds(r0, lc), g * LANES:(g + 1) * LANES] = y_diag + y_off
            chunk_decay = jnp.where(row_i[:, 0:1] < HEAD_DIM, jnp.exp(tot[:, j0:j0 + 1]), jnp.exp(tot[:, j1:j1 + 1]))
            h_s[direction * 2 + g] = h_in * chunk_decay + states

    h_s[...] = jnp.zeros_like(h_s)

    def scan_segment(base_chunk, n_seg):
        def body(t, carry):
            rf = pl.multiple_of((base_chunk + t) * lc, lc)
            rb = pl.multiple_of((base_chunk + n_seg - 1 - t) * lc, lc)
            scan_chunk(rf, 0, yf_s)
            scan_chunk(rb, 1, yb_s)
            return carry
        lax.fori_loop(0, n_seg, body, 0)

    scan_segment(n_lat, n_ctx)
    scan_segment(0, n_lat)

    def fin_body(c, carry):
        r0 = pl.multiple_of(c * lc, lc)
        xs = xbc_s[pl.ds(r0, lc), 0:GROUP_W]
        y = yf_s[pl.ds(r0, lc), :] + yb_s[pl.ds(r0, lc), :] + xs * dsk_ref[...]
        z = pb_ref[pl.ds(r0, lc), 0:GROUP_W].astype(F32)
        gte = y * (z * _sigmoid(z))
        ms = jnp.mean(gte * gte, axis=-1, keepdims=True)
        y_ref[pl.ds(r0, lc), :] = (gte * lax.rsqrt(ms + NORM_EPS) * nw_ref[...]).astype(BF16)
        return carry

    lax.fori_loop(0, n_chunks, fin_body, 0)


def _ssd_call(pb, pdt, cw, cb, dtb, alog, dsk, nw):
    b = pb.shape[0]
    return pl.pallas_call(
        _ssd_kernel,
        grid=(b,),
        in_specs=[
            pl.BlockSpec((None, TT, PB_W), lambda bi: (bi, 0, 0)),
            pl.BlockSpec((None, TT, PDT_W), lambda bi: (bi, 0, 0)),
            _resident((3, SSM_CONV_CH)),
            _resident((1, SSM_CONV_CH)),
            _resident((1, LANES)),
            _resident((1, LANES)),
            _resident((1, GROUP_W)),
            _resident((1, GROUP_W)),
        ],
        out_specs=pl.BlockSpec((None, TT, GROUP_W), lambda bi: (bi, 0, 0)),
        out_shape=jax.ShapeDtypeStruct((b, TT, GROUP_W), BF16),
        scratch_shapes=[
            pltpu.VMEM((TT, SSM_CONV_CH), F32),
            pltpu.VMEM((TT, GROUP_W), F32),
            pltpu.VMEM((TT, GROUP_W), F32),
            pltpu.VMEM((4, LANES, SSM_STATE), F32),
        ],
        compiler_params=_params(48, 1),
        name="ssd",
    )(pb, pdt, cw, cb, dtb, alog, dsk, nw)


def _out_kernel(ya_ref, yb_ref, yw_ref, ym_ref, x_ref, m_ref, w_ref, o_ref):
    acc = _dot(ya_ref[...], w_ref[0 * GROUP_W:1 * GROUP_W, :])
    acc += _dot(yb_ref[...], w_ref[1 * GROUP_W:2 * GROUP_W, :])
    acc += _dot(yw_ref[...], w_ref[2 * GROUP_W:3 * GROUP_W, :])
    acc += _dot(ym_ref[...], w_ref[3 * GROUP_W:4 * GROUP_W, :])
    o_ref[...] = x_ref[...] + m_ref[2:3, :] * acc


def _out_call(ys, x, mods, w, is_ctx):
    b, t, _ = x.shape
    n_tiles = t // TOK_TILE
    off = N_LAT_TILES if is_ctx else 0
    mi = 1 if is_ctx else 0
    y_spec = pl.BlockSpec((None, TOK_TILE, GROUP_W), lambda bi, i: (bi, i + off, 0))
    return pl.pallas_call(
        _out_kernel,
        grid=(b, n_tiles),
        in_specs=[y_spec, y_spec, y_spec, y_spec,
                  pl.BlockSpec((None, TOK_TILE, D_MODEL), lambda bi, i: (bi, i, 0)),
                  pl.BlockSpec((None, None, 8, D_MODEL), lambda bi, i: (bi, mi, 0, 0)),
                  _resident((4 * GROUP_W, D_MODEL))],
        out_specs=pl.BlockSpec((None, TOK_TILE, D_MODEL), lambda bi, i: (bi, i, 0)),
        out_shape=jax.ShapeDtypeStruct((b, t, D_MODEL), F32),
        compiler_params=_params(32),
        name="out_ctx" if is_ctx else "out_lat",
    )(*ys, x, mods, w)


def _ffn_kernel(xp_ref, x_ref, xn_ref, nw_ref, m_ref, wa_ref, wg_ref, cw_ref, cb_ref, wd_ref, fw_ref, o_ref,
                u_s, *, tm, final_norm):
    i = pl.program_id(1)
    n_i = pl.num_programs(1)
    x = x_ref[...]
    xa = jnp.concatenate([x, xp_ref[...], xn_ref[...]], axis=0)
    ms = jnp.mean(xa * xa, axis=-1, keepdims=True)
    y = xa * lax.rsqrt(ms + NORM_EPS) * nw_ref[...]
    h = (y * (1.0 + m_ref[4:5, :]) + m_ref[3:4, :]).astype(BF16)
    h_mid = h[0:tm, :]
    rid = lax.broadcasted_iota(jnp.int32, (tm, FFN_CHUNK), 0)
    has_prev = i > 0
    has_next = i < n_i - 1

    for j in range(D_FF // FFN_CHUNK):
        c0 = j * FFN_CHUNK
        a = _dot(h_mid, wa_ref[:, pl.ds(c0, FFN_CHUNK)])
        g = _dot(h, wg_ref[:, pl.ds(c0, FFN_CHUNK)])
        g_mid = g[0:tm, :]
        row_before = jnp.where(has_prev, g[tm + FFN_HALO - 1:tm + FFN_HALO, :], 0.0)
        row_after = jnp.where(has_next, g[tm + FFN_HALO:tm + FFN_HALO + 1, :], 0.0)
        g_prev = jnp.where(rid == 0, row_before, pltpu.roll(g_mid, 1, 0))
        g_next = jnp.where(rid == tm - 1, row_after, pltpu.roll(g_mid, tm - 1, 0))
        cw = cw_ref[:, pl.ds(c0, FFN_CHUNK)]
        gc = cw[0:1, :] * g_prev + cw[1:2, :] * g_mid + cw[2:3, :] * g_next + cb_ref[:, pl.ds(c0, FFN_CHUNK)]
        u_s[:, pl.ds(c0, FFN_CHUNK)] = (a * (gc * _sigmoid(gc))).astype(BF16)

    out = x + m_ref[5:6, :] * _dot(u_s[...], wd_ref[...])
    if final_norm:
        ms2 = jnp.mean(out * out, axis=-1, keepdims=True)
        out = out * lax.rsqrt(ms2 + NORM_EPS) * fw_ref[...]
    o_ref[...] = out


def _ffn_call(x, norm_w, mods, wa, wg, cw, cb, wd, fw, tm, is_ctx, final_norm):
    b, t, _ = x.shape
    n_tiles = t // tm
    hb = tm // FFN_HALO
    n_hblk = t // FFN_HALO
    mi = 1 if is_ctx else 0
    return pl.pallas_call(
        functools.partial(_ffn_kernel, tm=tm, final_norm=final_norm),
        grid=(b, n_tiles),
        in_specs=[
            pl.BlockSpec((None, FFN_HALO, D_MODEL), lambda bi, i: (bi, jnp.maximum(i * hb - 1, 0), 0)),
            pl.BlockSpec((None, tm, D_MODEL), lambda bi, i: (bi, i, 0)),
            pl.BlockSpec((None, FFN_HALO, D_MODEL), lambda bi, i: (bi, jnp.minimum((i + 1) * hb, n_hblk - 1), 0)),
            _resident((1, D_MODEL)),
            pl.BlockSpec((None, None, 8, D_MODEL), lambda bi, i: (bi, mi, 0, 0)),
            _resident((D_MODEL, D_FF)),
            _resident((D_MODEL, D_FF)),
            _resident((3, D_FF)),
            _resident((1, D_FF)),
            _resident((D_FF, D_MODEL)),
            _resident((1, D_MODEL)),
        ],
        out_specs=pl.BlockSpec((None, tm, D_MODEL), lambda bi, i: (bi, i, 0)),
        out_shape=jax.ShapeDtypeStruct((b, t, D_MODEL), F32),
        scratch_shapes=[pltpu.VMEM((tm, D_FF), BF16)],
        compiler_params=_params(56),
        name="ffn_ctx" if is_ctx else "ffn_lat",
    )(x, x, x, norm_w, mods, wa, wg, cw, cb, wd, fw)


def _pad_cols(w, width):
    return jnp.pad(w, ((0, 0), (0, width - w.shape[1])))


def _proj_weight(w_in_l):
    a_cols = 512
    b_cols = GROUP_W + SSM_CONV_CH + 8
    c_cols = 512
    wa = w_in_l[:, :a_cols]
    wb = w_in_l[:, a_cols:a_cols + b_cols]
    wc = w_in_l[:, a_cols + b_cols:a_cols + b_cols + c_cols]
    wd = w_in_l[:, a_cols + b_cols + c_cols:]
    w_zx = wb[:, :GROUP_W + SSM_CONV_CH]
    w_dt = _pad_cols(wb[:, GROUP_W + SSM_CONV_CH:], PDT_W)
    w_cq = _pad_cols(wd[:, :MLA_Q_RANK], 2 * LANES)
    w_ckv = wd[:, MLA_Q_RANK:MLA_Q_RANK + MLA_KV_RANK]
    w_kr = wd[:, MLA_Q_RANK + MLA_KV_RANK:]
    zeros = lambda n: jnp.zeros((D_MODEL, n), w_in_l.dtype)
    w_kr = jnp.concatenate([zeros(MLA_NOPE), w_kr, zeros(LANES - MLA_NOPE - MLA_ROPE)], axis=1)
    return jnp.concatenate([wa, w_zx, w_dt, wc, w_cq, w_ckv, w_kr], axis=1).astype(BF16)


def _mla_weights(w_uq_l, w_ukv_l):
    dq = MLA_NOPE + MLA_ROPE
    wq = w_uq_l.reshape(MLA_Q_RANK, 4, dq)
    wq = jnp.pad(wq, ((0, 2 * LANES - MLA_Q_RANK), (0, 0), (0, LANES - dq))).reshape(2 * LANES, 4 * LANES)
    wkv = w_ukv_l.reshape(MLA_KV_RANK, 4, MLA_NOPE + HEAD_DIM)
    wk = jnp.pad(wkv[:, :, :MLA_NOPE], ((0, 0), (0, 0), (0, LANES - MLA_NOPE))).reshape(MLA_KV_RANK, 4 * LANES)
    wv = wkv[:, :, MLA_NOPE:].reshape(MLA_KV_RANK, 4 * HEAD_DIM)
    return wq.astype(BF16), jnp.concatenate([wk, wv], axis=1).astype(BF16)


def kernel(x, c, ctx, c_ctx, norm1_w, w_mod, b_mod, w_in, attn_q_norm, attn_k_norm, ssm_conv_w, ssm_conv_b,
           ssm_dt_bias, ssm_a_log, ssm_d, ssm_norm_w, win_sink, mla_q_norm, mla_w_uq, mla_kv_norm, mla_w_ukv,
           w_out, norm2_w, ffn_w_up, ffn_conv_w, ffn_conv_b, ffn_w_down, final_norm_w):
    bsz = x.shape[0]
    depth = w_in.shape[0]
    assert x.shape[1:] == (SEQ, D_MODEL) and ctx.shape[1:] == (CTX_LEN, D_MODEL)

    tables = _axial_tables(HEAD_DIM, 0, 2) + _axial_tables(MLA_ROPE, MLA_NOPE, 1)

    n_rows = ((bsz + 1 + 7) // 8) * 8
    cvec = jnp.concatenate([c, c_ctx[None, :], jnp.zeros((n_rows - bsz - 1, D_MODEL), F32)], axis=0)
    mods_all = _mod_call(cvec, w_mod, b_mod.reshape(depth, 1, 6 * D_MODEL))

    x_lat, x_ctx = x, ctx
    for l in range(depth):
        with_ctx = l < depth - 1
        m_lat = mods_all[l, :bsz].reshape(bsz, 6, D_MODEL)
        m_ctx = jnp.broadcast_to(mods_all[l, bsz].reshape(1, 6, D_MODEL), (bsz, 6, D_MODEL))
        mods = jnp.pad(jnp.stack([m_lat, m_ctx], axis=1), ((0, 0), (0, 0), (0, 2), (0, 0)))

        wq, wkv = _mla_weights(mla_w_uq[l], mla_w_ukv[l])
        qa, ka, va, pb, pdt, qc, kc, vc, qd, kd, vd = _proj_call(
            x_lat, x_ctx, norm1_w[l].reshape(1, D_MODEL), mods, _proj_weight(w_in[l]), tables,
            jnp.tile(attn_q_norm[l], 2).reshape(1, LANES), jnp.tile(attn_k_norm[l], 2).reshape(1, LANES),
            _pad_cols(mla_q_norm[l].reshape(1, MLA_Q_RANK), 2 * LANES), mla_kv_norm[l].reshape(1, MLA_KV_RANK),
            wq, wkv)

        ya = _dense_attn_call(qa, ka, va, None, (0, 0, 0, 0), with_ctx, "attn_gqa")
        ym = _dense_attn_call(qd, kd, vd, None, (0, 1, 2, 3), with_ctx, "attn_mla")
        yw = _win_attn_call(qc, kc, vc, win_sink[l], with_ctx)
        yb = _ssd_call(
            pb, pdt, ssm_conv_w[l].T, ssm_conv_b[l].reshape(1, SSM_CONV_CH),
            _pad_cols(ssm_dt_bias[l].reshape(1, 8), LANES), _pad_cols(ssm_a_log[l].reshape(1, 8), LANES),
            jnp.repeat(ssm_d[l], HEAD_DIM).reshape(1, GROUP_W), ssm_norm_w[l].reshape(1, GROUP_W))

        ys = (ya, yb, yw, ym)
        w_o = w_out[l].astype(BF16)
        wa_up = ffn_w_up[l][:, :D_FF].astype(BF16)
        wg_up = ffn_w_up[l][:, D_FF:].astype(BF16)
        cw = ffn_conv_w[l].T
        cb = ffn_conv_b[l].reshape(1, D_FF)
        wd = ffn_w_down[l].astype(BF16)
        n2 = norm2_w[l].reshape(1, D_MODEL)
        fw = final_norm_w.reshape(1, D_MODEL)
        last = l == depth - 1

        x1 = _out_call(ys, x_lat, mods, w_o, False)
        x_lat = _ffn_call(x1, n2, mods, wa_up, wg_up, cw, cb, wd, fw, 512, False, last)
        if with_ctx:
            xc1 = _out_call(ys, x_ctx, mods, w_o, True)
            x_ctx = _ffn_call(xc1, n2, mods, wa_up, wg_up, cw, cb, wd, fw, CTX_LEN, True, False)
    return x_lat
```

```python
import functools

import numpy as np
import jax
import jax.numpy as jnp
from jax import lax
from jax.experimental import pallas as pl
from jax.experimental.pallas import tpu as pltpu

F32 = jnp.float32
BF16 = jnp.bfloat16

D_MODEL = 1024
SEQ = 2048
CTX_LEN = 256
TT = SEQ + CTX_LEN
GRID_W = 64
ROPE_THETA = 10000.0
NORM_EPS = 1e-6
HEAD_DIM = 64
GROUP_W = 256
WINDOW = 128
SSM_STATE = 128
SSM_CONV_CH = 768
MLA_NOPE = 64
MLA_ROPE = 32
MLA_Q_RANK = 192
MLA_KV_RANK = 128
D_FF = 2816

LANES = 128
BF16_ROWS = 16
TOK_TILE = 256
N_LAT_TILES = SEQ // TOK_TILE
N_TILES = TT // TOK_TILE
SSD_CHUNK = 128
FFN_CHUNK = 256
FFN_HALO = 16
LOG2E = 1.4426950408889634

PA_W, PB_W, PDT_W, PC_W, PD_W = 512, 1024, 128, 512, 512
OFF_A = 0
OFF_B = OFF_A + PA_W
OFF_DT = OFF_B + PB_W
OFF_C = OFF_DT + PDT_W
OFF_D = OFF_C + PC_W
PROJ_W = OFF_D + PD_W


def _resident(shape):
    return pl.BlockSpec(shape, lambda *_: (0,) * len(shape), pipeline_mode=pl.Buffered(1))


def _params(vmem_mb, ndims=2):
    return pltpu.CompilerParams(
        dimension_semantics=("arbitrary",) * ndims,
        vmem_limit_bytes=vmem_mb * 1024 * 1024,
    )


def _sigmoid(v):
    return 1.0 / (1.0 + jnp.exp(-v))


def _dot(a, b):
    return jnp.dot(a, b, preferred_element_type=F32)


def _dot_nt(a, b):
    return lax.dot_general(a, b, (((1,), (1,)), ((), ())), preferred_element_type=F32)


def _split3(v):
    t1 = v.astype(BF16)
    r1 = v - t1.astype(F32)
    t2 = r1.astype(BF16)
    t3 = (r1 - t2.astype(F32)).astype(BF16)
    return t1, t2, t3


def _mod_kernel(c_ref, w_ref, b_ref, o_ref):
    a = c_ref[...]
    s = a * _sigmoid(a)
    w = w_ref[...]
    s_hi = s.astype(BF16)
    s_lo = (s - s_hi.astype(F32)).astype(BF16)
    w_hi = w.astype(BF16)
    w_lo = (w - w_hi.astype(F32)).astype(BF16)
    o_ref[...] = _dot(s_hi, w_hi) + _dot(s_hi, w_lo) + _dot(s_lo, w_hi) + b_ref[...]


def _mod_call(cvec, w_mod, b_mod):
    n_layers, d, n6 = w_mod.shape
    rows = cvec.shape[0]
    tn = 1024
    return pl.pallas_call(
        _mod_kernel,
        grid=(n_layers, n6 // tn),
        in_specs=[
            pl.BlockSpec((rows, d), lambda l, j: (0, 0)),
            pl.BlockSpec((None, d, tn), lambda l, j: (l, 0, j)),
            pl.BlockSpec((None, 1, tn), lambda l, j: (l, 0, j)),
        ],
        out_specs=pl.BlockSpec((None, rows, tn), lambda l, j: (l, 0, j)),
        out_shape=jax.ShapeDtypeStruct((n_layers, rows, n6), F32),
        compiler_params=_params(40),
        name="mod",
    )(cvec, w_mod, b_mod)


def _rope(x, cos, sin_signed, half):
    lane = lax.broadcasted_iota(jnp.int32, x.shape, 1)
    first = (lane & (2 * half - 1)) < half
    partner = jnp.where(first, pltpu.roll(x, LANES - half, 1), pltpu.roll(x, half, 1))
    return x * cos + partner * sin_signed


def _axial_tables(rot_dim, lane_lo, reps):
    half = rot_dim // 2
    t = np.arange(SEQ)
    row = (t // GRID_W).astype(np.float32)
    col = (t % GRID_W).astype(np.float32)
    inv_freq = np.power(np.float32(ROPE_THETA), -np.arange(0, half, 2, dtype=np.float32) / np.float32(half))
    inv_freq = inv_freq.astype(np.float32)
    ang_r = (row[:, None] * inv_freq[None, :]).astype(np.float32)
    ang_c = (col[:, None] * inv_freq[None, :]).astype(np.float32)
    ang = np.concatenate([ang_r, ang_r, ang_c, ang_c], axis=-1).astype(np.float64)
    cos = np.cos(ang)
    sin = np.sin(ang)
    quarter = half // 2
    sign = np.where((np.arange(rot_dim) % half) < quarter, -1.0, 1.0)
    cos_t = np.ones((TT, LANES), np.float64)
    sin_t = np.zeros((TT, LANES), np.float64)
    for r in range(reps):
        lo = lane_lo + r * rot_dim
        cos_t[:SEQ, lo:lo + rot_dim] = cos
        sin_t[:SEQ, lo:lo + rot_dim] = sin * sign[None, :]
    return jnp.asarray(cos_t, F32), jnp.asarray(sin_t, F32)


def _store_value_slabs(pair, swapped, lo, v_ref):
    v_ref[0] = jnp.where(lo, pair, 1.0).astype(BF16)
    v_ref[1] = jnp.where(lo, 1.0, swapped).astype(BF16)
    v_ref[2] = jnp.where(lo, swapped, 1.0).astype(BF16)
    v_ref[3] = jnp.where(lo, 1.0, pair).astype(BF16)


def _gqa_prep(p, cos, sin, qn, kn, do_norm, q_ref, k_ref, v_ref):
    lane = lax.broadcasted_iota(jnp.int32, (p.shape[0], LANES), 1)
    lo = lane < HEAD_DIM

    def head_norm(v, gain):
        v2 = v * v
        s_lo = jnp.sum(jnp.where(lo, v2, 0.0), axis=-1, keepdims=True)
        s_hi = jnp.sum(jnp.where(lo, 0.0, v2), axis=-1, keepdims=True)
        ms = jnp.where(lo, s_lo, s_hi) * (1.0 / HEAD_DIM)
        return v * lax.rsqrt(ms + NORM_EPS) * gain

    scale = HEAD_DIM ** -0.5 * LOG2E
    for j in range(2):
        v = p[:, j * LANES:(j + 1) * LANES]
        if do_norm:
            v = head_norm(v, qn)
        v = _rope(v, cos, sin, HEAD_DIM // 4) * scale
        sw = pltpu.roll(v, HEAD_DIM, 1)
        if j == 0:
            q_ref[:, 0:LANES] = jnp.where(lo, v, 0.0).astype(BF16)
            q_ref[:, LANES:2 * LANES] = jnp.where(lo, sw, 0.0).astype(BF16)
        else:
            q_ref[:, 2 * LANES:3 * LANES] = jnp.where(lo, 0.0, sw).astype(BF16)
            q_ref[:, 3 * LANES:4 * LANES] = jnp.where(lo, 0.0, v).astype(BF16)

    k = p[:, 2 * LANES:3 * LANES]
    if do_norm:
        k = head_norm(k, kn)
    k_ref[0] = _rope(k, cos, sin, HEAD_DIM // 4).astype(BF16)

    vv = p[:, 3 * LANES:4 * LANES]
    _store_value_slabs(vv, pltpu.roll(vv, HEAD_DIM, 1), lo, v_ref)


def _mla_prep(p, cos, sin, gq, gkv, wq_ref, wkv_ref, q_ref, k_ref, v_ref):
    half = MLA_ROPE // 4
    cq = p[:, 0:2 * LANES]
    ms = jnp.sum(cq * cq, axis=-1, keepdims=True) * (1.0 / MLA_Q_RANK)
    cqn = (cq * lax.rsqrt(ms + NORM_EPS) * gq).astype(BF16)
    q = _dot(cqn, wq_ref[...])
    scale = (MLA_NOPE + MLA_ROPE) ** -0.5 * LOG2E
    for h in range(4):
        qh = _rope(q[:, h * LANES:(h + 1) * LANES], cos, sin, half) * scale
        q_ref[:, h * LANES:(h + 1) * LANES] = qh.astype(BF16)

    ckv = p[:, 2 * LANES:3 * LANES]
    ms = jnp.mean(ckv * ckv, axis=-1, keepdims=True)
    ckvn = (ckv * lax.rsqrt(ms + NORM_EPS) * gkv).astype(BF16)
    kv = _dot(ckvn, wkv_ref[...])
    kr = _rope(p[:, 3 * LANES:4 * LANES], cos, sin, half)
    for h in range(4):
        k_ref[h] = (kv[:, h * LANES:(h + 1) * LANES] + kr).astype(BF16)
    lane = lax.broadcasted_iota(jnp.int32, (p.shape[0], LANES), 1)
    lo = lane < HEAD_DIM
    v01 = kv[:, 4 * LANES:5 * LANES]
    v23 = kv[:, 5 * LANES:6 * LANES]
    v_ref[0] = jnp.where(lo, v01, 1.0).astype(BF16)
    v_ref[1] = jnp.where(lo, 1.0, v01).astype(BF16)
    v_ref[2] = jnp.where(lo, v23, 1.0).astype(BF16)
    v_ref[3] = jnp.where(lo, 1.0, v23).astype(BF16)


def _proj_kernel(xl_ref, xc_ref, nw_ref, m_ref, w_ref, cosa_ref, sina_ref, cosm_ref, sinm_ref,
                 qn_ref, kn_ref, gq_ref, gkv_ref, wq_ref, wkv_ref,
                 qa_ref, ka_ref, va_ref, pb_ref, pdt_ref, qc_ref, kc_ref, vc_ref, qd_ref, kd_ref, vd_ref):
    i = pl.program_id(1)
    x = jnp.where(i < N_LAT_TILES, xl_ref[...], xc_ref[...])
    ms = jnp.mean(x * x, axis=-1, keepdims=True)
    y = x * lax.rsqrt(ms + NORM_EPS) * nw_ref[...]
    h = (y * (1.0 + m_ref[1:2, :]) + m_ref[0:1, :]).astype(BF16)
    cosa = cosa_ref[...]
    sina = sina_ref[...]
    _mla_prep(_dot(h, w_ref[:, OFF_D:OFF_D + PD_W]), cosm_ref[...], sinm_ref[...], gq_ref[...], gkv_ref[...],
              wq_ref, wkv_ref, qd_ref, kd_ref, vd_ref)
    _gqa_prep(_dot(h, w_ref[:, OFF_A:OFF_A + PA_W]), cosa, sina, qn_ref[...], kn_ref[...], True,
              qa_ref, ka_ref, va_ref)
    _gqa_prep(_dot(h, w_ref[:, OFF_C:OFF_C + PC_W]), cosa, sina, None, None, False,
              qc_ref, kc_ref, vc_ref)
    pdt_ref[...] = _dot(h, w_ref[:, OFF_DT:OFF_DT + PDT_W])
    pb_ref[...] = _dot(h, w_ref[:, OFF_B:OFF_B + PB_W]).astype(BF16)


def _proj_call(x_lat, x_ctx, norm_w, mods, w, tables, qn, kn, gq, gkv, wq, wkv):
    b = x_lat.shape[0]
    tok = lambda width: pl.BlockSpec((None, TOK_TILE, width), lambda bi, i: (bi, i, 0))
    slab = lambda n: pl.BlockSpec((None, n, TOK_TILE, LANES), lambda bi, i: (bi, 0, i, 0))
    table = pl.BlockSpec((TOK_TILE, LANES), lambda bi, i: (i, 0))
    q_shape = jax.ShapeDtypeStruct((b, TT, 4 * LANES), BF16)
    slab_shape = lambda n: jax.ShapeDtypeStruct((b, n, TT, LANES), BF16)
    return pl.pallas_call(
        _proj_kernel,
        grid=(b, N_TILES),
        in_specs=[
            pl.BlockSpec((None, TOK_TILE, D_MODEL), lambda bi, i: (bi, jnp.minimum(i, N_LAT_TILES - 1), 0)),
            pl.BlockSpec((None, TOK_TILE, D_MODEL), lambda bi, i: (bi, 0, 0)),
            _resident((1, D_MODEL)),
            pl.BlockSpec((None, None, 8, D_MODEL), lambda bi, i: (bi, i // N_LAT_TILES, 0, 0)),
            _resident((D_MODEL, PROJ_W)),
            table, table, table, table,
            _resident((1, LANES)), _resident((1, LANES)),
            _resident((1, 2 * LANES)), _resident((1, LANES)),
            _resident((2 * LANES, 4 * LANES)), _resident((LANES, 6 * LANES)),
        ],
        out_specs=[tok(4 * LANES), slab(1), slab(4), tok(PB_W), tok(PDT_W),
                   tok(4 * LANES), slab(1), slab(4), tok(4 * LANES), slab(4), slab(4)],
        out_shape=[q_shape, slab_shape(1), slab_shape(4),
                   jax.ShapeDtypeStruct((b, TT, PB_W), BF16), jax.ShapeDtypeStruct((b, TT, PDT_W), F32),
                   q_shape, slab_shape(1), slab_shape(4), q_shape, slab_shape(4), slab_shape(4)],
        compiler_params=_params(48),
        name="proj",
    )(x_lat, x_ctx, norm_w, mods, w, *tables, qn, kn, gq, gkv, wq, wkv)


def _softmax_pv(s_parts, v_parts, sink):
    m = s_parts[0].max(axis=-1, keepdims=True)
    for s in s_parts[1:]:
        m = jnp.maximum(m, s.max(axis=-1, keepdims=True))
    if sink is not None:
        m = jnp.maximum(m, sink)
    acc = None
    for s, v in zip(s_parts, v_parts):
        o = _dot(jnp.exp2(s - m).astype(BF16), v)
        acc = o if acc is None else acc + o
    den = pltpu.roll(acc, HEAD_DIM, 1)
    if sink is not None:
        den = den + jnp.exp2(sink - m)
    return acc / den


def _merge_heads(outs, o_ref):
    lane = lax.broadcasted_iota(jnp.int32, outs[0].shape, 1)
    lo = lane < HEAD_DIM
    o_ref[:, 0:LANES] = jnp.where(lo, outs[0], outs[1]).astype(BF16)
    o_ref[:, LANES:2 * LANES] = jnp.where(lo, outs[2], outs[3]).astype(BF16)


def _dense_attn_kernel(*refs, k_of, has_sink, with_ctx):
    if has_sink:
        sink_ref, q_ref, k_ref, v_ref, o_ref = refs
    else:
        q_ref, k_ref, v_ref, o_ref = refs
        sink_ref = None
    i = pl.program_id(1)

    def run(k_lo, k_hi):
        def scores(h):
            return _dot_nt(q_ref[:, h * LANES:(h + 1) * LANES], k_ref[k_of[h], k_lo:k_hi, :])

        outs = []
        s_next = scores(0)
        for h in range(4):
            s = s_next
            if h < 3:
                s_next = scores(h + 1)
            sink = sink_ref[h] * LOG2E if has_sink else None
            outs.append(_softmax_pv([s], [v_ref[h, k_lo:k_hi, :]], sink))
        _merge_heads(outs, o_ref)

    if with_ctx:
        @pl.when(i < N_LAT_TILES)
        def _():
            run(0, TT)

        @pl.when(i >= N_LAT_TILES)
        def _():
            run(SEQ, TT)
    else:
        run(0, TT)


def _dense_attn_call(q, k, v, sink, k_of, with_ctx, name):
    b = q.shape[0]
    n_k = k.shape[1]
    n_tiles = N_TILES if with_ctx else N_LAT_TILES
    has_sink = sink is not None
    in_specs = [
        pl.BlockSpec((None, TOK_TILE, 4 * LANES), lambda bi, i: (bi, i, 0)),
        pl.BlockSpec((None, n_k, TT, LANES), lambda bi, i: (bi, 0, 0, 0)),
        pl.BlockSpec((None, 4, TT, LANES), lambda bi, i: (bi, 0, 0, 0)),
    ]
    args = [q, k, v]
    if has_sink:
        in_specs = [pl.BlockSpec(memory_space=pltpu.SMEM)] + in_specs
        args = [sink] + args
    return pl.pallas_call(
        functools.partial(_dense_attn_kernel, k_of=k_of, has_sink=has_sink, with_ctx=with_ctx),
        grid=(b, n_tiles),
        in_specs=in_specs,
        out_specs=pl.BlockSpec((None, TOK_TILE, GROUP_W), lambda bi, i: (bi, i, 0)),
        out_shape=jax.ShapeDtypeStruct((b, TT, GROUP_W), BF16),
        compiler_params=_params(48),
        name=name,
    )(*args)


WIN_TILE = 256
WIN_BAND = WIN_TILE + 2 * WINDOW


def _win_attn_kernel(sink_ref, q_ref, k_ref, v_ref, o_ref, *, with_ctx):
    i = pl.program_id(1)
    n_lat = SEQ // WIN_TILE

    def run_band():
        start = jnp.clip(i * WIN_TILE - WINDOW, 0, SEQ - WIN_BAND)
        start = pl.multiple_of(start, WINDOW)
        qpos = i * WIN_TILE + lax.broadcasted_iota(jnp.int32, (WIN_TILE, WIN_BAND), 0)
        kpos = start + lax.broadcasted_iota(jnp.int32, (WIN_TILE, WIN_BAND), 1)
        mask = jnp.abs(kpos - qpos) <= WINDOW
        k_band = k_ref[0, pl.ds(start, WIN_BAND), :]
        k_ctx = k_ref[0, SEQ:TT, :]
        def scores(h):
            qh = q_ref[:, h * LANES:(h + 1) * LANES]
            return [jnp.where(mask, _dot_nt(qh, k_band), -jnp.inf), _dot_nt(qh, k_ctx)]

        outs = []
        s_next = scores(0)
        for h in range(4):
            s_parts = s_next
            if h < 3:
                s_next = scores(h + 1)
            outs.append(_softmax_pv(
                s_parts,
                [v_ref[h, pl.ds(start, WIN_BAND), :], v_ref[h, SEQ:TT, :]],
                sink_ref[h] * LOG2E))
        _merge_heads(outs, o_ref)

    def run_ctx():
        k_ctx = k_ref[0, SEQ:TT, :]
        outs = []
        for h in range(4):
            qh = q_ref[:, h * LANES:(h + 1) * LANES]
            outs.append(_softmax_pv([_dot_nt(qh, k_ctx)], [v_ref[h, SEQ:TT, :]], sink_ref[h] * LOG2E))
        _merge_heads(outs, o_ref)

    if with_ctx:
        pl.when(i < n_lat)(run_band)
        pl.when(i >= n_lat)(run_ctx)
    else:
        run_band()


def _win_attn_call(q, k, v, sink, with_ctx):
    b = q.shape[0]
    n_tiles = (TT if with_ctx else SEQ) // WIN_TILE
    return pl.pallas_call(
        functools.partial(_win_attn_kernel, with_ctx=with_ctx),
        grid=(b, n_tiles),
        in_specs=[
            pl.BlockSpec(memory_space=pltpu.SMEM),
            pl.BlockSpec((None, WIN_TILE, 4 * LANES), lambda bi, i: (bi, i, 0)),
            pl.BlockSpec((None, 1, TT, LANES), lambda bi, i: (bi, 0, 0, 0)),
            pl.BlockSpec((None, 4, TT, LANES), lambda bi, i: (bi, 0, 0, 0)),
        ],
        out_specs=pl.BlockSpec((None, WIN_TILE, GROUP_W), lambda bi, i: (bi, i, 0)),
        out_shape=jax.ShapeDtypeStruct((b, TT, GROUP_W), BF16),
        compiler_params=_params(32),
        name="attn_win",
    )(sink, q, k, v)


def _ssd_kernel(pb_ref, dt_ref, cw_ref, cb_ref, dtb_ref, alog_ref, dsk_ref, nw_ref, y_ref,
                xbc_s, st_s, yd_s, csc_s, dt_s, cs_s, w_s, ecs_s, etot_s, h_s):
    lc = SSD_CHUNK
    n_chunks = TT // lc
    n_lat = SEQ // lc
    n_ctx = CTX_LEN // lc
    row_i = lax.broadcasted_iota(jnp.int32, (lc, lc), 0)
    col_i = lax.broadcasted_iota(jnp.int32, (lc, lc), 1)
    top = row_i < HEAD_DIM
    keep_f = row_i <= col_i
    keep_b = row_i >= col_i

    def pair_rows(tab, j0):
        return jnp.where(top, tab[j0:j0 + 1, :], tab[j0 + 1:j0 + 2, :])

    pre = jnp.concatenate([dt_ref[c * lc:(c + 1) * lc, :].T[0:8, :] for c in range(n_chunks)], axis=0) + dtb_ref[...]
    dt_all = jnp.maximum(pre, 0.0) + jnp.log1p(jnp.exp(-jnp.abs(pre)))
    da = dt_all * (-jnp.exp(alog_ref[...]))
    lane_t = lax.broadcasted_iota(jnp.int32, da.shape, 1)
    row_t = lax.broadcasted_iota(jnp.int32, da.shape, 0)
    cs = da
    k = 1
    while k < lc:
        cs = cs + jnp.where(lane_t >= k, pltpu.roll(cs, k, 1), 0.0)
        k *= 2
    tot = cs[:, lc - 1:lc]
    cs_all = jnp.where((row_t & 7) < 4, cs, tot - cs + da)
    dt_s[...] = dt_all
    cs_s[...] = cs_all
    w_s[...] = jnp.exp(tot - cs_all)
    ecs_s[...] = jnp.exp(cs_all)
    etot_s[...] = jnp.broadcast_to(jnp.exp(tot), da.shape)
    for c in range(n_chunks):
        csc_s[c] = jnp.concatenate([cs_all[c * 8:(c + 1) * 8, :]] * (lc // 8), axis=0).T

    def chunk_a(c):
        r0 = pl.multiple_of(c * lc, lc)
        t0 = pl.multiple_of(c * 8, 8)
        first = jnp.logical_or(c == 0, c == n_lat)
        last = jnp.logical_or(c == n_lat - 1, c == n_chunks - 1)
        u = pb_ref[pl.ds(r0, lc), GROUP_W:GROUP_W + SSM_CONV_CH].astype(F32)
        rp = pl.multiple_of(jnp.maximum(r0 - BF16_ROWS, 0), BF16_ROWS)
        rn = pl.multiple_of(jnp.minimum(r0 + lc, TT - BF16_ROWS), BF16_ROWS)
        prev_blk = pb_ref[pl.ds(rp, BF16_ROWS), GROUP_W:GROUP_W + SSM_CONV_CH].astype(F32)
        next_blk = pb_ref[pl.ds(rn, BF16_ROWS), GROUP_W:GROUP_W + SSM_CONV_CH].astype(F32)
        prev_row = jnp.where(first, 0.0, prev_blk[BF16_ROWS - 1:BF16_ROWS, :])
        next_row = jnp.where(last, 0.0, next_blk[0:1, :])
        rows = lax.broadcasted_iota(jnp.int32, (lc, SSM_CONV_CH), 0)
        up = jnp.where(rows == 0, prev_row, pltpu.roll(u, 1, 0))
        un = jnp.where(rows == lc - 1, next_row, pltpu.roll(u, lc - 1, 0))
        v = cw_ref[0:1, :] * up + cw_ref[1:2, :] * u + cw_ref[2:3, :] * un + cb_ref[...]
        act = v * _sigmoid(v)
        xbc_s[pl.ds(r0, lc), :] = act
        xs = act[:, 0:GROUP_W]
        bm = act[:, GROUP_W:2 * GROUP_W]
        cm = act[:, 2 * GROUP_W:3 * GROUP_W]
        dt_t = dt_s[pl.ds(t0, 8), :]
        cs_t = cs_s[pl.ds(t0, 8), :]
        w_t = w_s[pl.ds(t0, 8), :]
        cs_cols = csc_s[c]

        for g in range(2):
            bm_g = bm[:, g * LANES:(g + 1) * LANES].astype(BF16)
            cm_g = cm[:, g * LANES:(g + 1) * LANES].astype(BF16)
            cb_t = _dot_nt(bm_g, cm_g)
            xs_t = xs[:, g * LANES:(g + 1) * LANES].T
            yd = None
            for d in range(2):
                j0 = d * 4 + 2 * g
                keep = keep_f if d == 0 else keep_b
                xdt_t = xs_t * pair_rows(dt_t, j0)
                xdt_b = xdt_t.astype(BF16)
                parts = []
                for hh in range(2):
                    j = j0 + hh
                    seg_t = cs_t[j:j + 1, :] - cs_cols[:, j:j + 1]
                    dec_t = jnp.exp(jnp.where(keep, seg_t, -jnp.inf))
                    parts.append(_dot(xdt_b[hh * HEAD_DIM:(hh + 1) * HEAD_DIM, :], (cb_t * dec_t).astype(BF16)))
                y_dg = jnp.concatenate(parts, axis=0)
                yd = y_dg if yd is None else yd + y_dg
                xw_t = xdt_t * pair_rows(w_t, j0)
                st_s[c * 4 + d * 2 + g] = _dot(xw_t.astype(BF16), bm_g)
            yd_s[c * 2 + g] = yd

    def phase_a(t, carry):
        chunk_a(2 * t)
        chunk_a(2 * t + 1)
        return carry

    lax.fori_loop(0, n_chunks // 2, phase_a, 0)

    h_s[...] = jnp.zeros_like(h_s)

    def phase_b(base_chunk, n_seg):
        def body(t, carry):
            for d in range(2):
                c = base_chunk + t if d == 0 else base_chunk + n_seg - 1 - t
                e = etot_s[pl.ds(pl.multiple_of(c * 8, 8), 8), :]
                for g in range(2):
                    slot = c * 4 + d * 2 + g
                    s_c = st_s[slot]
                    h_in = h_s[d * 2 + g]
                    st_s[slot] = h_in
                    h_s[d * 2 + g] = h_in * pair_rows(e, d * 4 + 2 * g) + s_c
            return carry
        lax.fori_loop(0, n_seg, body, 0)

    phase_b(n_lat, n_ctx)
    phase_b(0, n_lat)

    def chunk_c(c):
        r0 = pl.multiple_of(c * lc, lc)
        xs = xbc_s[pl.ds(r0, lc), 0:GROUP_W]
        cm = xbc_s[pl.ds(r0, lc), 2 * GROUP_W:3 * GROUP_W]
        ecs = ecs_s[pl.ds(pl.multiple_of(c * 8, 8), 8), :]
        halves = []
        for g in range(2):
            cm_g = cm[:, g * LANES:(g + 1) * LANES].astype(BF16)
            y_t = yd_s[c * 2 + g]
            for d in range(2):
                h_in = st_s[c * 4 + d * 2 + g].astype(BF16)
                y_t = y_t + _dot_nt(h_in, cm_g) * pair_rows(ecs, d * 4 + 2 * g)
            halves.append(y_t.T)
        y = jnp.concatenate(halves, axis=1) + xs * dsk_ref[...]
        z = pb_ref[pl.ds(r0, lc), 0:GROUP_W].astype(F32)
        gte = y * (z * _sigmoid(z))
        ms = jnp.mean(gte * gte, axis=-1, keepdims=True)
        y_ref[pl.ds(r0, lc), :] = (gte * lax.rsqrt(ms + NORM_EPS) * nw_ref[...]).astype(BF16)

    def phase_c(t, carry):
        for u in range(3):
            chunk_c(3 * t + u)
        return carry

    lax.fori_loop(0, n_chunks // 3, phase_c, 0)


def _ssd_call(pb, pdt, cw, cb, dtb, alog, dsk, nw):
    b = pb.shape[0]
    n_chunks = TT // SSD_CHUNK
    return pl.pallas_call(
        _ssd_kernel,
        grid=(b,),
        in_specs=[
            pl.BlockSpec((None, TT, PB_W), lambda bi: (bi, 0, 0)),
            pl.BlockSpec((None, TT, PDT_W), lambda bi: (bi, 0, 0)),
            _resident((3, SSM_CONV_CH)),
            _resident((1, SSM_CONV_CH)),
            _resident((n_chunks * 8, SSD_CHUNK)),
            _resident((n_chunks * 8, SSD_CHUNK)),
            _resident((1, GROUP_W)),
            _resident((1, GROUP_W)),
        ],
        out_specs=pl.BlockSpec((None, TT, GROUP_W), lambda bi: (bi, 0, 0)),
        out_shape=jax.ShapeDtypeStruct((b, TT, GROUP_W), BF16),
        scratch_shapes=[
            pltpu.VMEM((TT, SSM_CONV_CH), F32),
            pltpu.VMEM((n_chunks * 4, LANES, SSM_STATE), F32),
            pltpu.VMEM((n_chunks * 2, LANES, SSD_CHUNK), F32),
            pltpu.VMEM((n_chunks, SSD_CHUNK, LANES), F32),
            pltpu.VMEM((n_chunks * 8, SSD_CHUNK), F32),
            pltpu.VMEM((n_chunks * 8, SSD_CHUNK), F32),
            pltpu.VMEM((n_chunks * 8, SSD_CHUNK), F32),
            pltpu.VMEM((n_chunks * 8, SSD_CHUNK), F32),
            pltpu.VMEM((n_chunks * 8, SSD_CHUNK), F32),
            pltpu.VMEM((4, LANES, SSM_STATE), F32),
        ],
        compiler_params=_params(48, 1),
        name="ssd",
    )(pb, pdt, cw, cb, dtb, alog, dsk, nw)


def _ffn_kernel(*refs, tm, final_norm):
    y_refs = refs[0:12]
    x_ref, xp_ref, xn_ref, m_ref, wo_ref, nw_ref, wa_ref, wg_ref, cw_ref, cb_ref, wd_ref, fw_ref = refs[12:24]
    o_ref, u_s = refs[24:26]
    i = pl.program_id(1)
    n_i = pl.num_programs(1)
    hl = FFN_HALO
    acc = None
    for j in range(4):
        ycat = jnp.concatenate([r[...] for r in y_refs[3 * j:3 * j + 3]], axis=0)
        d = _dot(ycat, wo_ref[j * GROUP_W:(j + 1) * GROUP_W, :])
        acc = d if acc is None else acc + d
    x1 = jnp.concatenate([x_ref[...], xp_ref[...], xn_ref[...]], axis=0) + m_ref[2:3, :] * acc
    ms = jnp.mean(x1 * x1, axis=-1, keepdims=True)
    y = x1 * lax.rsqrt(ms + NORM_EPS) * nw_ref[...]
    h = (y * (1.0 + m_ref[4:5, :]) + m_ref[3:4, :]).astype(BF16)
    h_mid = h[0:tm, :]
    rid = lax.broadcasted_iota(jnp.int32, (tm, FFN_CHUNK), 0)
    has_prev = i > 0
    has_next = i < n_i - 1

    for j in range(D_FF // FFN_CHUNK):
        c0 = j * FFN_CHUNK
        a = _dot(h_mid, wa_ref[:, pl.ds(c0, FFN_CHUNK)])
        g = _dot(h, wg_ref[:, pl.ds(c0, FFN_CHUNK)])
        g_mid = g[0:tm, :]
        row_before = jnp.where(has_prev, g[tm + hl - 1:tm + hl, :], 0.0)
        row_after = jnp.where(has_next, g[tm + hl:tm + hl + 1, :], 0.0)
        g_prev = jnp.where(rid == 0, row_before, pltpu.roll(g_mid, 1, 0))
        g_next = jnp.where(rid == tm - 1, row_after, pltpu.roll(g_mid, tm - 1, 0))
        cw = cw_ref[:, pl.ds(c0, FFN_CHUNK)]
        gc = cw[0:1, :] * g_prev + cw[1:2, :] * g_mid + cw[2:3, :] * g_next + cb_ref[:, pl.ds(c0, FFN_CHUNK)]
        u_s[:, pl.ds(c0, FFN_CHUNK)] = (a * (gc * _sigmoid(gc))).astype(BF16)

    out = x1[0:tm, :] + m_ref[5:6, :] * _dot(u_s[...], wd_ref[...])
    if final_norm:
        ms2 = jnp.mean(out * out, axis=-1, keepdims=True)
        out = out * lax.rsqrt(ms2 + NORM_EPS) * fw_ref[...]
    o_ref[...] = out


def _ffn_call(ys, x, mods, wo, norm_w, wa, wg, cw, cb, wd, fw, tm, is_ctx, final_norm):
    b, t, _ = x.shape
    n_tiles = t // tm
    hl = FFN_HALO
    hb = tm // hl
    row0 = SEQ if is_ctx else 0
    mi = 1 if is_ctx else 0

    def triple(width, rows, base):
        off_m = base // tm
        off_h = base // hl
        last = rows // hl - 1
        return [
            pl.BlockSpec((None, tm, width), lambda bi, i: (bi, i + off_m, 0)),
            pl.BlockSpec((None, hl, width), lambda bi, i: (bi, jnp.maximum(i * hb + off_h - 1, 0), 0)),
            pl.BlockSpec((None, hl, width), lambda bi, i: (bi, jnp.minimum((i + 1) * hb + off_h, last), 0)),
        ]

    in_specs = []
    args = []
    for yv in ys:
        in_specs += triple(GROUP_W, TT, row0)
        args += [yv, yv, yv]
    in_specs += triple(D_MODEL, t, 0)
    args += [x, x, x]
    in_specs += [
        pl.BlockSpec((None, None, 8, D_MODEL), lambda bi, i: (bi, mi, 0, 0)),
        _resident((4 * GROUP_W, D_MODEL)),
        _resident((1, D_MODEL)),
        _resident((D_MODEL, D_FF)),
        _resident((D_MODEL, D_FF)),
        _resident((3, D_FF)),
        _resident((1, D_FF)),
        _resident((D_FF, D_MODEL)),
        _resident((1, D_MODEL)),
    ]
    args += [mods, wo, norm_w, wa, wg, cw, cb, wd, fw]
    return pl.pallas_call(
        functools.partial(_ffn_kernel, tm=tm, final_norm=final_norm),
        grid=(b, n_tiles),
        in_specs=in_specs,
        out_specs=pl.BlockSpec((None, tm, D_MODEL), lambda bi, i: (bi, i, 0)),
        out_shape=jax.ShapeDtypeStruct((b, t, D_MODEL), F32),
        scratch_shapes=[pltpu.VMEM((tm, D_FF), BF16)],
        compiler_params=_params(56),
        name="ffn_ctx" if is_ctx else "ffn_lat",
    )(*args)


def _pad_cols(w, width):
    return jnp.pad(w, ((0, 0), (0, width - w.shape[1])))


def _proj_weight(w_in_l):
    a_cols = 512
    b_cols = GROUP_W + SSM_CONV_CH + 8
    c_cols = 512
    wa = w_in_l[:, :a_cols]
    wb = w_in_l[:, a_cols:a_cols + b_cols]
    wc = w_in_l[:, a_cols + b_cols:a_cols + b_cols + c_cols]
    wd = w_in_l[:, a_cols + b_cols + c_cols:]
    w_zx = wb[:, :GROUP_W + SSM_CONV_CH]
    w_dt = _pad_cols(wb[:, GROUP_W + SSM_CONV_CH:], PDT_W)
    w_cq = _pad_cols(wd[:, :MLA_Q_RANK], 2 * LANES)
    w_ckv = wd[:, MLA_Q_RANK:MLA_Q_RANK + MLA_KV_RANK]
    w_kr = wd[:, MLA_Q_RANK + MLA_KV_RANK:]
    zeros = lambda n: jnp.zeros((D_MODEL, n), w_in_l.dtype)
    w_kr = jnp.concatenate([zeros(MLA_NOPE), w_kr, zeros(LANES - MLA_NOPE - MLA_ROPE)], axis=1)
    return jnp.concatenate([wa, w_zx, w_dt, wc, w_cq, w_ckv, w_kr], axis=1).astype(BF16)


def _mla_weights(w_uq_l, w_ukv_l):
    dq = MLA_NOPE + MLA_ROPE
    wq = w_uq_l.reshape(MLA_Q_RANK, 4, dq)
    wq = jnp.pad(wq, ((0, 2 * LANES - MLA_Q_RANK), (0, 0), (0, LANES - dq))).reshape(2 * LANES, 4 * LANES)
    wkv = w_ukv_l.reshape(MLA_KV_RANK, 4, MLA_NOPE + HEAD_DIM)
    wk = jnp.pad(wkv[:, :, :MLA_NOPE], ((0, 0), (0, 0), (0, LANES - MLA_NOPE))).reshape(MLA_KV_RANK, 4 * LANES)
    wv = wkv[:, :, MLA_NOPE:].reshape(MLA_KV_RANK, 4 * HEAD_DIM)
    return wq.astype(BF16), jnp.concatenate([wk, wv], axis=1).astype(BF16)


def kernel(x, c, ctx, c_ctx, norm1_w, w_mod, b_mod, w_in, attn_q_norm, attn_k_norm, ssm_conv_w, ssm_conv_b,
           ssm_dt_bias, ssm_a_log, ssm_d, ssm_norm_w, win_sink, mla_q_norm, mla_w_uq, mla_kv_norm, mla_w_ukv,
           w_out, norm2_w, ffn_w_up, ffn_conv_w, ffn_conv_b, ffn_w_down, final_norm_w):
    bsz = x.shape[0]
    depth = w_in.shape[0]
    assert x.shape[1:] == (SEQ, D_MODEL) and ctx.shape[1:] == (CTX_LEN, D_MODEL)

    tables = _axial_tables(HEAD_DIM, 0, 2) + _axial_tables(MLA_ROPE, MLA_NOPE, 1)

    n_rows = ((bsz + 1 + 7) // 8) * 8
    cvec = jnp.concatenate([c, c_ctx[None, :], jnp.zeros((n_rows - bsz - 1, D_MODEL), F32)], axis=0)
    mods_all = _mod_call(cvec, w_mod, b_mod.reshape(depth, 1, 6 * D_MODEL))

    x_lat, x_ctx = x, ctx
    for l in range(depth):
        with_ctx = l < depth - 1
        m_lat = mods_all[l, :bsz].reshape(bsz, 6, D_MODEL)
        m_ctx = jnp.broadcast_to(mods_all[l, bsz].reshape(1, 6, D_MODEL), (bsz, 6, D_MODEL))
        mods = jnp.pad(jnp.stack([m_lat, m_ctx], axis=1), ((0, 0), (0, 0), (0, 2), (0, 0)))

        wq, wkv = _mla_weights(mla_w_uq[l], mla_w_ukv[l])
        qa, ka, va, pb, pdt, qc, kc, vc, qd, kd, vd = _proj_call(
            x_lat, x_ctx, norm1_w[l].reshape(1, D_MODEL), mods, _proj_weight(w_in[l]), tables,
            jnp.tile(attn_q_norm[l], 2).reshape(1, LANES), jnp.tile(attn_k_norm[l], 2).reshape(1, LANES),
            _pad_cols(mla_q_norm[l].reshape(1, MLA_Q_RANK), 2 * LANES), mla_kv_norm[l].reshape(1, MLA_KV_RANK),
            wq, wkv)

        ya = _dense_attn_call(qa, ka, va, None, (0, 0, 0, 0), with_ctx, "attn_gqa")
        ym = _dense_attn_call(qd, kd, vd, None, (0, 1, 2, 3), with_ctx, "attn_mla")
        yw = _win_attn_call(qc, kc, vc, win_sink[l], with_ctx)
        yb = _ssd_call(
            pb, pdt, ssm_conv_w[l].T, ssm_conv_b[l].reshape(1, SSM_CONV_CH),
            jnp.tile(jnp.broadcast_to(ssm_dt_bias[l].reshape(8, 1), (8, SSD_CHUNK)), (TT // SSD_CHUNK, 1)),
            jnp.tile(jnp.broadcast_to(ssm_a_log[l].reshape(8, 1), (8, SSD_CHUNK)), (TT // SSD_CHUNK, 1)),
            jnp.repeat(ssm_d[l], HEAD_DIM).reshape(1, GROUP_W), ssm_norm_w[l].reshape(1, GROUP_W))

        ys = (ya, yb, yw, ym)
        w_o = w_out[l].astype(BF16)
        wa_up = ffn_w_up[l][:, :D_FF].astype(BF16)
        wg_up = ffn_w_up[l][:, D_FF:].astype(BF16)
        cw = ffn_conv_w[l].T
        cb = ffn_conv_b[l].reshape(1, D_FF)
        wd = ffn_w_down[l].astype(BF16)
        n2 = norm2_w[l].reshape(1, D_MODEL)
        fw = final_norm_w.reshape(1, D_MODEL)
        last = l == depth - 1

        x_lat_next = _ffn_call(ys, x_lat, mods, w_o, n2, wa_up, wg_up, cw, cb, wd, fw, 512, False, last)
        if with_ctx:
            x_ctx = _ffn_call(ys, x_ctx, mods, w_o, n2, wa_up, wg_up, cw, cb, wd, fw, CTX_LEN, True, False)
        x_lat = x_lat_next
    return x_lat
```

```python
import functools

import numpy as np
import jax
import jax.numpy as jnp
from jax import lax
from jax.experimental import pallas as pl
from jax.experimental.pallas import tpu as pltpu

F32 = jnp.float32
BF16 = jnp.bfloat16

D_MODEL = 1024
SEQ = 2048
CTX_LEN = 256
TT = SEQ + CTX_LEN
GRID_W = 64
ROPE_THETA = 10000.0
NORM_EPS = 1e-6
HEAD_DIM = 64
GROUP_W = 256
WINDOW = 128
SSM_STATE = 128
SSM_CONV_CH = 768
MLA_NOPE = 64
MLA_ROPE = 32
MLA_Q_RANK = 192
MLA_KV_RANK = 128
D_FF = 2816

LANES = 128
BF16_ROWS = 16
TOK_TILE = 256
N_LAT_TILES = SEQ // TOK_TILE
N_TILES = TT // TOK_TILE
SSD_CHUNK = 128
FFN_CHUNK = 256
FFN_HALO = 16
LOG2E = 1.4426950408889634

PA_W, PB_W, PDT_W, PC_W, PD_W = 512, 1024, 128, 512, 512
OFF_A = 0
OFF_B = OFF_A + PA_W
OFF_DT = OFF_B + PB_W
OFF_C = OFF_DT + PDT_W
OFF_D = OFF_C + PC_W
PROJ_W = OFF_D + PD_W


def _resident(shape):
    return pl.BlockSpec(shape, lambda *_: (0,) * len(shape), pipeline_mode=pl.Buffered(1))


def _params(vmem_mb, ndims=2):
    return pltpu.CompilerParams(
        dimension_semantics=("arbitrary",) * ndims,
        vmem_limit_bytes=vmem_mb * 1024 * 1024,
    )


def _sigmoid(v):
    return 1.0 / (1.0 + jnp.exp(-v))


def _dot(a, b):
    return jnp.dot(a, b, preferred_element_type=F32)


def _dot_nt(a, b):
    return lax.dot_general(a, b, (((1,), (1,)), ((), ())), preferred_element_type=F32)


def _split3(v):
    t1 = v.astype(BF16)
    r1 = v - t1.astype(F32)
    t2 = r1.astype(BF16)
    t3 = (r1 - t2.astype(F32)).astype(BF16)
    return t1, t2, t3


def _mod_kernel(c_ref, w_ref, b_ref, o_ref):
    a = c_ref[...]
    s = a * _sigmoid(a)
    w = w_ref[...]
    s_hi = s.astype(BF16)
    s_lo = (s - s_hi.astype(F32)).astype(BF16)
    w_hi = w.astype(BF16)
    w_lo = (w - w_hi.astype(F32)).astype(BF16)
    o_ref[...] = _dot(s_hi, w_hi) + _dot(s_hi, w_lo) + _dot(s_lo, w_hi) + b_ref[...]


def _mod_call(cvec, w_mod, b_mod):
    n_layers, d, n6 = w_mod.shape
    rows = cvec.shape[0]
    tn = 1024
    return pl.pallas_call(
        _mod_kernel,
        grid=(n_layers, n6 // tn),
        in_specs=[
            pl.BlockSpec((rows, d), lambda l, j: (0, 0)),
            pl.BlockSpec((None, d, tn), lambda l, j: (l, 0, j)),
            pl.BlockSpec((None, 1, tn), lambda l, j: (l, 0, j)),
        ],
        out_specs=pl.BlockSpec((None, rows, tn), lambda l, j: (l, 0, j)),
        out_shape=jax.ShapeDtypeStruct((n_layers, rows, n6), F32),
        compiler_params=_params(40),
        name="mod",
    )(cvec, w_mod, b_mod)


def _rope(x, cos, sin_signed, half):
    lane = lax.broadcasted_iota(jnp.int32, x.shape, 1)
    first = (lane & (2 * half - 1)) < half
    partner = jnp.where(first, pltpu.roll(x, LANES - half, 1), pltpu.roll(x, half, 1))
    return x * cos + partner * sin_signed


def _axial_tables(rot_dim, lane_lo, reps):
    half = rot_dim // 2
    t = np.arange(SEQ)
    row = (t // GRID_W).astype(np.float32)
    col = (t % GRID_W).astype(np.float32)
    inv_freq = np.power(np.float32(ROPE_THETA), -np.arange(0, half, 2, dtype=np.float32) / np.float32(half))
    inv_freq = inv_freq.astype(np.float32)
    ang_r = (row[:, None] * inv_freq[None, :]).astype(np.float32)
    ang_c = (col[:, None] * inv_freq[None, :]).astype(np.float32)
    ang = np.concatenate([ang_r, ang_r, ang_c, ang_c], axis=-1).astype(np.float64)
    cos = np.cos(ang)
    sin = np.sin(ang)
    quarter = half // 2
    sign = np.where((np.arange(rot_dim) % half) < quarter, -1.0, 1.0)
    cos_t = np.ones((TT, LANES), np.float64)
    sin_t = np.zeros((TT, LANES), np.float64)
    for r in range(reps):
        lo = lane_lo + r * rot_dim
        cos_t[:SEQ, lo:lo + rot_dim] = cos
        sin_t[:SEQ, lo:lo + rot_dim] = sin * sign[None, :]
    return jnp.asarray(cos_t, F32), jnp.asarray(sin_t, F32)


def _store_value_slabs(pair, swapped, lo, v_ref):
    v_ref[0] = jnp.where(lo, pair, 1.0).astype(BF16)
    v_ref[1] = jnp.where(lo, 1.0, swapped).astype(BF16)
    v_ref[2] = jnp.where(lo, swapped, 1.0).astype(BF16)
    v_ref[3] = jnp.where(lo, 1.0, pair).astype(BF16)


def _gqa_prep(p, cos, sin, qn, kn, do_norm, q_ref, k_ref, v_ref):
    lane = lax.broadcasted_iota(jnp.int32, (p.shape[0], LANES), 1)
    lo = lane < HEAD_DIM

    def head_norm(v, gain):
        v2 = v * v
        s_lo = jnp.sum(jnp.where(lo, v2, 0.0), axis=-1, keepdims=True)
        s_hi = jnp.sum(jnp.where(lo, 0.0, v2), axis=-1, keepdims=True)
        ms = jnp.where(lo, s_lo, s_hi) * (1.0 / HEAD_DIM)
        return v * lax.rsqrt(ms + NORM_EPS) * gain

    scale = HEAD_DIM ** -0.5 * LOG2E
    for j in range(2):
        v = p[:, j * LANES:(j + 1) * LANES]
        if do_norm:
            v = head_norm(v, qn)
        v = _rope(v, cos, sin, HEAD_DIM // 4) * scale
        sw = pltpu.roll(v, HEAD_DIM, 1)
        if j == 0:
            q_ref[:, 0:LANES] = jnp.where(lo, v, 0.0).astype(BF16)
            q_ref[:, LANES:2 * LANES] = jnp.where(lo, sw, 0.0).astype(BF16)
        else:
            q_ref[:, 2 * LANES:3 * LANES] = jnp.where(lo, 0.0, sw).astype(BF16)
            q_ref[:, 3 * LANES:4 * LANES] = jnp.where(lo, 0.0, v).astype(BF16)

    k = p[:, 2 * LANES:3 * LANES]
    if do_norm:
        k = head_norm(k, kn)
    k_ref[0] = _rope(k, cos, sin, HEAD_DIM // 4).astype(BF16)

    vv = p[:, 3 * LANES:4 * LANES]
    _store_value_slabs(vv, pltpu.roll(vv, HEAD_DIM, 1), lo, v_ref)


def _mla_prep(p, cos, sin, gq, gkv, wq_ref, wkv_ref, q_ref, k_ref, v_ref):
    half = MLA_ROPE // 4
    cq = p[:, 0:2 * LANES]
    ms = jnp.sum(cq * cq, axis=-1, keepdims=True) * (1.0 / MLA_Q_RANK)
    cqn = (cq * lax.rsqrt(ms + NORM_EPS) * gq).astype(BF16)
    q = _dot(cqn, wq_ref[...])
    scale = (MLA_NOPE + MLA_ROPE) ** -0.5 * LOG2E
    for h in range(4):
        qh = _rope(q[:, h * LANES:(h + 1) * LANES], cos, sin, half) * scale
        q_ref[:, h * LANES:(h + 1) * LANES] = qh.astype(BF16)

    ckv = p[:, 2 * LANES:3 * LANES]
    ms = jnp.mean(ckv * ckv, axis=-1, keepdims=True)
    ckvn = (ckv * lax.rsqrt(ms + NORM_EPS) * gkv).astype(BF16)
    kv = _dot(ckvn, wkv_ref[...])
    kr = _rope(p[:, 3 * LANES:4 * LANES], cos, sin, half)
    for h in range(4):
        k_ref[h] = (kv[:, h * LANES:(h + 1) * LANES] + kr).astype(BF16)
    lane = lax.broadcasted_iota(jnp.int32, (p.shape[0], LANES), 1)
    lo = lane < HEAD_DIM
    v01 = kv[:, 4 * LANES:5 * LANES]
    v23 = kv[:, 5 * LANES:6 * LANES]
    v_ref[0] = jnp.where(lo, v01, 1.0).astype(BF16)
    v_ref[1] = jnp.where(lo, 1.0, v01).astype(BF16)
    v_ref[2] = jnp.where(lo, v23, 1.0).astype(BF16)
    v_ref[3] = jnp.where(lo, 1.0, v23).astype(BF16)


def _proj_kernel(xl_ref, xc_ref, nw_ref, m_ref, w_ref, cosa_ref, sina_ref, cosm_ref, sinm_ref,
                 qn_ref, kn_ref, gq_ref, gkv_ref, wq_ref, wkv_ref,
                 qa_ref, ka_ref, va_ref, pb_ref, pdt_ref, qc_ref, kc_ref, vc_ref, qd_ref, kd_ref, vd_ref):
    i = pl.program_id(1)
    x = jnp.where(i < N_LAT_TILES, xl_ref[...], xc_ref[...])
    ms = jnp.mean(x * x, axis=-1, keepdims=True)
    y = x * lax.rsqrt(ms + NORM_EPS) * nw_ref[...]
    h = (y * (1.0 + m_ref[1:2, :]) + m_ref[0:1, :]).astype(BF16)
    cosa = cosa_ref[...]
    sina = sina_ref[...]
    _mla_prep(_dot(h, w_ref[:, OFF_D:OFF_D + PD_W]), cosm_ref[...], sinm_ref[...], gq_ref[...], gkv_ref[...],
              wq_ref, wkv_ref, qd_ref, kd_ref, vd_ref)
    _gqa_prep(_dot(h, w_ref[:, OFF_A:OFF_A + PA_W]), cosa, sina, qn_ref[...], kn_ref[...], True,
              qa_ref, ka_ref, va_ref)
    _gqa_prep(_dot(h, w_ref[:, OFF_C:OFF_C + PC_W]), cosa, sina, None, None, False,
              qc_ref, kc_ref, vc_ref)
    pdt_ref[...] = _dot(h, w_ref[:, OFF_DT:OFF_DT + PDT_W])
    pb_ref[...] = _dot(h, w_ref[:, OFF_B:OFF_B + PB_W]).astype(BF16)


def _proj_call(x_lat, x_ctx, norm_w, mods, w, tables, qn, kn, gq, gkv, wq, wkv):
    b = x_lat.shape[0]
    tok = lambda width: pl.BlockSpec((None, TOK_TILE, width), lambda bi, i: (bi, i, 0))
    slab = lambda n: pl.BlockSpec((None, n, TOK_TILE, LANES), lambda bi, i: (bi, 0, i, 0))
    table = pl.BlockSpec((TOK_TILE, LANES), lambda bi, i: (i, 0))
    q_shape = jax.ShapeDtypeStruct((b, TT, 4 * LANES), BF16)
    slab_shape = lambda n: jax.ShapeDtypeStruct((b, n, TT, LANES), BF16)
    return pl.pallas_call(
        _proj_kernel,
        grid=(b, N_TILES),
        in_specs=[
            pl.BlockSpec((None, TOK_TILE, D_MODEL), lambda bi, i: (bi, jnp.minimum(i, N_LAT_TILES - 1), 0)),
            pl.BlockSpec((None, TOK_TILE, D_MODEL), lambda bi, i: (bi, 0, 0)),
            _resident((1, D_MODEL)),
            pl.BlockSpec((None, None, 8, D_MODEL), lambda bi, i: (bi, i // N_LAT_TILES, 0, 0)),
            _resident((D_MODEL, PROJ_W)),
            table, table, table, table,
            _resident((1, LANES)), _resident((1, LANES)),
            _resident((1, 2 * LANES)), _resident((1, LANES)),
            _resident((2 * LANES, 4 * LANES)), _resident((LANES, 6 * LANES)),
        ],
        out_specs=[tok(4 * LANES), slab(1), slab(4), tok(PB_W), tok(PDT_W),
                   tok(4 * LANES), slab(1), slab(4), tok(4 * LANES), slab(4), slab(4)],
        out_shape=[q_shape, slab_shape(1), slab_shape(4),
                   jax.ShapeDtypeStruct((b, TT, PB_W), BF16), jax.ShapeDtypeStruct((b, TT, PDT_W), F32),
                   q_shape, slab_shape(1), slab_shape(4), q_shape, slab_shape(4), slab_shape(4)],
        compiler_params=_params(48),
        name="proj",
    )(x_lat, x_ctx, norm_w, mods, w, *tables, qn, kn, gq, gkv, wq, wkv)


def _softmax_pv(s_parts, v_parts, sink):
    m = s_parts[0].max(axis=-1, keepdims=True)
    for s in s_parts[1:]:
        m = jnp.maximum(m, s.max(axis=-1, keepdims=True))
    if sink is not None:
        m = jnp.maximum(m, sink)
    acc = None
    for s, v in zip(s_parts, v_parts):
        o = _dot(jnp.exp2(s - m).astype(BF16), v)
        acc = o if acc is None else acc + o
    den = pltpu.roll(acc, HEAD_DIM, 1)
    if sink is not None:
        den = den + jnp.exp2(sink - m)
    return acc / den


def _merge_heads(outs, store):
    lane = lax.broadcasted_iota(jnp.int32, outs[0].shape, 1)
    lo = lane < HEAD_DIM
    store(0, jnp.where(lo, outs[0], outs[1]).astype(BF16))
    store(1, jnp.where(lo, outs[2], outs[3]).astype(BF16))


ATT_TILE = 256
DENSE_TILE = 512


def _dense_attn_kernel(*refs, k_of, has_sink, with_ctx):
    if has_sink:
        sink_ref, q_ref, k_ref, v_ref, o_ref = refs
    else:
        q_ref, k_ref, v_ref, o_ref = refs
        sink_ref = None

    def tile(r0, rows, k_lo, k_hi):
        def scores(h):
            return _dot_nt(q_ref[pl.ds(r0, rows), h * LANES:(h + 1) * LANES], k_ref[k_of[h], k_lo:k_hi, :])

        def store(j, val):
            o_ref[pl.ds(r0, rows), j * LANES:(j + 1) * LANES] = val

        outs = []
        s_next = scores(0)
        for h in range(4):
            s = s_next
            if h < 3:
                s_next = scores(h + 1)
            sink = sink_ref[h] * LOG2E if has_sink else None
            outs.append(_softmax_pv([s], [v_ref[h, k_lo:k_hi, :]], sink))
        _merge_heads(outs, store)

    def body(t, carry):
        tile(pl.multiple_of(t * DENSE_TILE, DENSE_TILE), DENSE_TILE, 0, TT)
        return carry

    lax.fori_loop(0, SEQ // DENSE_TILE, body, 0)
    if with_ctx:
        tile(SEQ, CTX_LEN, SEQ, TT)
    else:
        o_ref[SEQ:TT, :] = jnp.zeros((CTX_LEN, GROUP_W), BF16)


def _dense_attn_call(q, k, v, sink, k_of, with_ctx, name):
    b = q.shape[0]
    n_k = k.shape[1]
    has_sink = sink is not None
    in_specs = [
        pl.BlockSpec((None, TT, 4 * LANES), lambda bi: (bi, 0, 0)),
        pl.BlockSpec((None, n_k, TT, LANES), lambda bi: (bi, 0, 0, 0)),
        pl.BlockSpec((None, 4, TT, LANES), lambda bi: (bi, 0, 0, 0)),
    ]
    args = [q, k, v]
    if has_sink:
        in_specs = [pl.BlockSpec(memory_space=pltpu.SMEM)] + in_specs
        args = [sink] + args
    return pl.pallas_call(
        functools.partial(_dense_attn_kernel, k_of=k_of, has_sink=has_sink, with_ctx=with_ctx),
        grid=(b,),
        in_specs=in_specs,
        out_specs=pl.BlockSpec((None, TT, GROUP_W), lambda bi: (bi, 0, 0)),
        out_shape=jax.ShapeDtypeStruct((b, TT, GROUP_W), BF16),
        compiler_params=_params(48, 1),
        name=name,
    )(*args)


WIN_BAND = ATT_TILE + 2 * WINDOW


def _win_attn_kernel(sink_ref, q_ref, k_ref, v_ref, o_ref, *, with_ctx):
    k_ctx = k_ref[0, SEQ:TT, :]

    def finish(r0, rows, outs):
        def store(j, val):
            o_ref[pl.ds(r0, rows), j * LANES:(j + 1) * LANES] = val
        _merge_heads(outs, store)

    def band_tile(t, carry):
        r0 = pl.multiple_of(t * ATT_TILE, ATT_TILE)
        start = pl.multiple_of(jnp.clip(r0 - WINDOW, 0, SEQ - WIN_BAND), WINDOW)
        qpos = r0 + lax.broadcasted_iota(jnp.int32, (ATT_TILE, WIN_BAND), 0)
        kpos = start + lax.broadcasted_iota(jnp.int32, (ATT_TILE, WIN_BAND), 1)
        mask = jnp.abs(kpos - qpos) <= WINDOW
        k_band = k_ref[0, pl.ds(start, WIN_BAND), :]

        def scores(h):
            qh = q_ref[pl.ds(r0, ATT_TILE), h * LANES:(h + 1) * LANES]
            return [jnp.where(mask, _dot_nt(qh, k_band), -jnp.inf), _dot_nt(qh, k_ctx)]

        outs = []
        s_next = scores(0)
        for h in range(4):
            s_parts = s_next
            if h < 3:
                s_next = scores(h + 1)
            outs.append(_softmax_pv(
                s_parts,
                [v_ref[h, pl.ds(start, WIN_BAND), :], v_ref[h, SEQ:TT, :]],
                sink_ref[h] * LOG2E))
        finish(r0, ATT_TILE, outs)
        return carry

    lax.fori_loop(0, SEQ // ATT_TILE, band_tile, 0)
    if with_ctx:
        outs = []
        for h in range(4):
            qh = q_ref[SEQ:TT, h * LANES:(h + 1) * LANES]
            outs.append(_softmax_pv([_dot_nt(qh, k_ctx)], [v_ref[h, SEQ:TT, :]], sink_ref[h] * LOG2E))
        finish(SEQ, CTX_LEN, outs)
    else:
        o_ref[SEQ:TT, :] = jnp.zeros((CTX_LEN, GROUP_W), BF16)


def _win_attn_call(q, k, v, sink, with_ctx):
    b = q.shape[0]
    return pl.pallas_call(
        functools.partial(_win_attn_kernel, with_ctx=with_ctx),
        grid=(b,),
        in_specs=[
            pl.BlockSpec(memory_space=pltpu.SMEM),
            pl.BlockSpec((None, TT, 4 * LANES), lambda bi: (bi, 0, 0)),
            pl.BlockSpec((None, 1, TT, LANES), lambda bi: (bi, 0, 0, 0)),
            pl.BlockSpec((None, 4, TT, LANES), lambda bi: (bi, 0, 0, 0)),
        ],
        out_specs=pl.BlockSpec((None, TT, GROUP_W), lambda bi: (bi, 0, 0)),
        out_shape=jax.ShapeDtypeStruct((b, TT, GROUP_W), BF16),
        compiler_params=_params(32, 1),
        name="attn_win",
    )(sink, q, k, v)


def _ssd_kernel(pb_ref, dt_ref, cw_ref, cb_ref, dtb_ref, alog_ref, dsk_ref, nw_ref, y_ref,
                xbc_s, st_s, yd_s, csc_s, dt_s, cs_s, w_s, ecs_s, etot_s, h_s):
    lc = SSD_CHUNK
    n_chunks = TT // lc
    n_lat = SEQ // lc
    n_ctx = CTX_LEN // lc
    row_i = lax.broadcasted_iota(jnp.int32, (lc, lc), 0)
    col_i = lax.broadcasted_iota(jnp.int32, (lc, lc), 1)
    top = row_i < HEAD_DIM
    keep_f = row_i <= col_i
    keep_b = row_i >= col_i

    def pair_rows(tab, j0):
        return jnp.where(top, tab[j0:j0 + 1, :], tab[j0 + 1:j0 + 2, :])

    pre = jnp.concatenate([dt_ref[c * lc:(c + 1) * lc, :].T[0:8, :] for c in range(n_chunks)], axis=0) + dtb_ref[...]
    dt_all = jnp.maximum(pre, 0.0) + jnp.log1p(jnp.exp(-jnp.abs(pre)))
    da = dt_all * (-jnp.exp(alog_ref[...]))
    lane_t = lax.broadcasted_iota(jnp.int32, da.shape, 1)
    row_t = lax.broadcasted_iota(jnp.int32, da.shape, 0)
    cs = da
    k = 1
    while k < lc:
        cs = cs + jnp.where(lane_t >= k, pltpu.roll(cs, k, 1), 0.0)
        k *= 2
    tot = cs[:, lc - 1:lc]
    cs_all = jnp.where((row_t & 7) < 4, cs, tot - cs + da)
    dt_s[...] = dt_all
    cs_s[...] = cs_all
    w_s[...] = jnp.exp(tot - cs_all)
    ecs_s[...] = jnp.exp(cs_all)
    etot_s[...] = jnp.broadcast_to(jnp.exp(tot), da.shape)
    for c in range(n_chunks):
        csc_s[c] = jnp.concatenate([cs_all[c * 8:(c + 1) * 8, :]] * (lc // 8), axis=0).T

    def chunk_a(c):
        r0 = pl.multiple_of(c * lc, lc)
        t0 = pl.multiple_of(c * 8, 8)
        first = jnp.logical_or(c == 0, c == n_lat)
        last = jnp.logical_or(c == n_lat - 1, c == n_chunks - 1)
        u = pb_ref[pl.ds(r0, lc), GROUP_W:GROUP_W + SSM_CONV_CH].astype(F32)
        rp = pl.multiple_of(jnp.maximum(r0 - BF16_ROWS, 0), BF16_ROWS)
        rn = pl.multiple_of(jnp.minimum(r0 + lc, TT - BF16_ROWS), BF16_ROWS)
        prev_blk = pb_ref[pl.ds(rp, BF16_ROWS), GROUP_W:GROUP_W + SSM_CONV_CH].astype(F32)
        next_blk = pb_ref[pl.ds(rn, BF16_ROWS), GROUP_W:GROUP_W + SSM_CONV_CH].astype(F32)
        prev_row = jnp.where(first, 0.0, prev_blk[BF16_ROWS - 1:BF16_ROWS, :])
        next_row = jnp.where(last, 0.0, next_blk[0:1, :])
        rows = lax.broadcasted_iota(jnp.int32, (lc, SSM_CONV_CH), 0)
        up = jnp.where(rows == 0, prev_row, pltpu.roll(u, 1, 0))
        un = jnp.where(rows == lc - 1, next_row, pltpu.roll(u, lc - 1, 0))
        v = cw_ref[0:1, :] * up + cw_ref[1:2, :] * u + cw_ref[2:3, :] * un + cb_ref[...]
        act = v * _sigmoid(v)
        xbc_s[pl.ds(r0, lc), :] = act
        xs = act[:, 0:GROUP_W]
        bm = act[:, GROUP_W:2 * GROUP_W]
        cm = act[:, 2 * GROUP_W:3 * GROUP_W]
        dt_t = dt_s[pl.ds(t0, 8), :]
        cs_t = cs_s[pl.ds(t0, 8), :]
        w_t = w_s[pl.ds(t0, 8), :]
        cs_cols = csc_s[c]

        for g in range(2):
            bm_g = bm[:, g * LANES:(g + 1) * LANES].astype(BF16)
            cm_g = cm[:, g * LANES:(g + 1) * LANES].astype(BF16)
            cb_t = _dot_nt(bm_g, cm_g)
            xs_t = xs[:, g * LANES:(g + 1) * LANES].T
            yd = None
            for d in range(2):
                j0 = d * 4 + 2 * g
                keep = keep_f if d == 0 else keep_b
                xdt_t = xs_t * pair_rows(dt_t, j0)
                xdt_b = xdt_t.astype(BF16)
                parts = []
                for hh in range(2):
                    j = j0 + hh
                    seg_t = cs_t[j:j + 1, :] - cs_cols[:, j:j + 1]
                    dec_t = jnp.exp(jnp.where(keep, seg_t, -jnp.inf))
                    parts.append(_dot(xdt_b[hh * HEAD_DIM:(hh + 1) * HEAD_DIM, :], (cb_t * dec_t).astype(BF16)))
                y_dg = jnp.concatenate(parts, axis=0)
                yd = y_dg if yd is None else yd + y_dg
                xw_t = xdt_t * pair_rows(w_t, j0)
                st_s[c * 4 + d * 2 + g] = _dot(xw_t.astype(BF16), bm_g)
            yd_s[c * 2 + g] = yd

    def phase_a(t, carry):
        chunk_a(2 * t)
        chunk_a(2 * t + 1)
        return carry

    lax.fori_loop(0, n_chunks // 2, phase_a, 0)

    h_s[...] = jnp.zeros_like(h_s)

    def phase_b(base_chunk, n_seg):
        def body(t, carry):
            for d in range(2):
                c = base_chunk + t if d == 0 else base_chunk + n_seg - 1 - t
                e = etot_s[pl.ds(pl.multiple_of(c * 8, 8), 8), :]
                for g in range(2):
                    slot = c * 4 + d * 2 + g
                    s_c = st_s[slot]
                    h_in = h_s[d * 2 + g]
                    st_s[slot] = h_in
                    h_s[d * 2 + g] = h_in * pair_rows(e, d * 4 + 2 * g) + s_c
            return carry
        lax.fori_loop(0, n_seg, body, 0)

    phase_b(n_lat, n_ctx)
    phase_b(0, n_lat)

    def chunk_c(c):
        r0 = pl.multiple_of(c * lc, lc)
        xs = xbc_s[pl.ds(r0, lc), 0:GROUP_W]
        cm = xbc_s[pl.ds(r0, lc), 2 * GROUP_W:3 * GROUP_W]
        ecs = ecs_s[pl.ds(pl.multiple_of(c * 8, 8), 8), :]
        halves = []
        for g in range(2):
            cm_g = cm[:, g * LANES:(g + 1) * LANES].astype(BF16)
            y_t = yd_s[c * 2 + g]
            for d in range(2):
                h_in = st_s[c * 4 + d * 2 + g].astype(BF16)
                y_t = y_t + _dot_nt(h_in, cm_g) * pair_rows(ecs, d * 4 + 2 * g)
            halves.append(y_t.T)
        y = jnp.concatenate(halves, axis=1) + xs * dsk_ref[...]
        z = pb_ref[pl.ds(r0, lc), 0:GROUP_W].astype(F32)
        gte = y * (z * _sigmoid(z))
        ms = jnp.mean(gte * gte, axis=-1, keepdims=True)
        y_ref[pl.ds(r0, lc), :] = (gte * lax.rsqrt(ms + NORM_EPS) * nw_ref[...]).astype(BF16)

    def phase_c(t, carry):
        for u in range(3):
            chunk_c(3 * t + u)
        return carry

    lax.fori_loop(0, n_chunks // 3, phase_c, 0)


def _ssd_call(pb, pdt, cw, cb, dtb, alog, dsk, nw):
    b = pb.shape[0]
    n_chunks = TT // SSD_CHUNK
    return pl.pallas_call(
        _ssd_kernel,
        grid=(b,),
        in_specs=[
            pl.BlockSpec((None, TT, PB_W), lambda bi: (bi, 0, 0)),
            pl.BlockSpec((None, TT, PDT_W), lambda bi: (bi, 0, 0)),
            _resident((3, SSM_CONV_CH)),
            _resident((1, SSM_CONV_CH)),
            _resident((n_chunks * 8, SSD_CHUNK)),
            _resident((n_chunks * 8, SSD_CHUNK)),
            _resident((1, GROUP_W)),
            _resident((1, GROUP_W)),
        ],
        out_specs=pl.BlockSpec((None, TT, GROUP_W), lambda bi: (bi, 0, 0)),
        out_shape=jax.ShapeDtypeStruct((b, TT, GROUP_W), BF16),
        scratch_shapes=[
            pltpu.VMEM((TT, SSM_CONV_CH), F32),
            pltpu.VMEM((n_chunks * 4, LANES, SSM_STATE), F32),
            pltpu.VMEM((n_chunks * 2, LANES, SSD_CHUNK), F32),
            pltpu.VMEM((n_chunks, SSD_CHUNK, LANES), F32),
            pltpu.VMEM((n_chunks * 8, SSD_CHUNK), F32),
            pltpu.VMEM((n_chunks * 8, SSD_CHUNK), F32),
            pltpu.VMEM((n_chunks * 8, SSD_CHUNK), F32),
            pltpu.VMEM((n_chunks * 8, SSD_CHUNK), F32),
            pltpu.VMEM((n_chunks * 8, SSD_CHUNK), F32),
            pltpu.VMEM((4, LANES, SSM_STATE), F32),
        ],
        compiler_params=_params(48, 1),
        name="ssd",
    )(pb, pdt, cw, cb, dtb, alog, dsk, nw)


def _ffn_kernel(*refs, tm, final_norm):
    y_refs = refs[0:12]
    x_ref, xp_ref, xn_ref, m_ref, wo_ref, nw_ref, wa_ref, wg_ref, cw_ref, cb_ref, wd_ref, fw_ref = refs[12:24]
    o_ref, u_s = refs[24:26]
    i = pl.program_id(1)
    n_i = pl.num_programs(1)
    hl = FFN_HALO
    acc = None
    for j in range(4):
        ycat = jnp.concatenate([r[...] for r in y_refs[3 * j:3 * j + 3]], axis=0)
        d = _dot(ycat, wo_ref[j * GROUP_W:(j + 1) * GROUP_W, :])
        acc = d if acc is None else acc + d
    x1 = jnp.concatenate([x_ref[...], xp_ref[...], xn_ref[...]], axis=0) + m_ref[2:3, :] * acc
    ms = jnp.mean(x1 * x1, axis=-1, keepdims=True)
    y = x1 * lax.rsqrt(ms + NORM_EPS) * nw_ref[...]
    h = (y * (1.0 + m_ref[4:5, :]) + m_ref[3:4, :]).astype(BF16)
    h_mid = h[0:tm, :]
    rid = lax.broadcasted_iota(jnp.int32, (tm, FFN_CHUNK), 0)
    has_prev = i > 0
    has_next = i < n_i - 1

    for j in range(D_FF // FFN_CHUNK):
        c0 = j * FFN_CHUNK
        a = _dot(h_mid, wa_ref[:, pl.ds(c0, FFN_CHUNK)])
        g = _dot(h, wg_ref[:, pl.ds(c0, FFN_CHUNK)])
        g_mid = g[0:tm, :]
        row_before = jnp.where(has_prev, g[tm + hl - 1:tm + hl, :], 0.0)
        row_after = jnp.where(has_next, g[tm + hl:tm + hl + 1, :], 0.0)
        g_prev = jnp.where(rid == 0, row_before, pltpu.roll(g_mid, 1, 0))
        g_next = jnp.where(rid == tm - 1, row_after, pltpu.roll(g_mid, tm - 1, 0))
        cw = cw_ref[:, pl.ds(c0, FFN_CHUNK)]
        gc = cw[0:1, :] * g_prev + cw[1:2, :] * g_mid + cw[2:3, :] * g_next + cb_ref[:, pl.ds(c0, FFN_CHUNK)]
        u_s[:, pl.ds(c0, FFN_CHUNK)] = (a * (gc * _sigmoid(gc))).astype(BF16)

    out = x1[0:tm, :] + m_ref[5:6, :] * _dot(u_s[...], wd_ref[...])
    if final_norm:
        ms2 = jnp.mean(out * out, axis=-1, keepdims=True)
        out = out * lax.rsqrt(ms2 + NORM_EPS) * fw_ref[...]
    o_ref[...] = out


def _ffn_call(ys, x, mods, wo, norm_w, wa, wg, cw, cb, wd, fw, tm, is_ctx, final_norm):
    b, t, _ = x.shape
    n_tiles = t // tm
    hl = FFN_HALO
    hb = tm // hl
    row0 = SEQ if is_ctx else 0
    mi = 1 if is_ctx else 0

    def triple(width, rows, base):
        off_m = base // tm
        off_h = base // hl
        last = rows // hl - 1
        return [
            pl.BlockSpec((None, tm, width), lambda bi, i: (bi, i + off_m, 0)),
            pl.BlockSpec((None, hl, width), lambda bi, i: (bi, jnp.maximum(i * hb + off_h - 1, 0), 0)),
            pl.BlockSpec((None, hl, width), lambda bi, i: (bi, jnp.minimum((i + 1) * hb + off_h, last), 0)),
        ]

    in_specs = []
    args = []
    for yv in ys:
        in_specs += triple(GROUP_W, TT, row0)
        args += [yv, yv, yv]
    in_specs += triple(D_MODEL, t, 0)
    args += [x, x, x]
    in_specs += [
        pl.BlockSpec((None, None, 8, D_MODEL), lambda bi, i: (bi, mi, 0, 0)),
        _resident((4 * GROUP_W, D_MODEL)),
        _resident((1, D_MODEL)),
        _resident((D_MODEL, D_FF)),
        _resident((D_MODEL, D_FF)),
        _resident((3, D_FF)),
        _resident((1, D_FF)),
        _resident((D_FF, D_MODEL)),
        _resident((1, D_MODEL)),
    ]
    args += [mods, wo, norm_w, wa, wg, cw, cb, wd, fw]
    return pl.pallas_call(
        functools.partial(_ffn_kernel, tm=tm, final_norm=final_norm),
        grid=(b, n_tiles),
        in_specs=in_specs,
        out_specs=pl.BlockSpec((None, tm, D_MODEL), lambda bi, i: (bi, i, 0)),
        out_shape=jax.ShapeDtypeStruct((b, t, D_MODEL), F32),
        scratch_shapes=[pltpu.VMEM((tm, D_FF), BF16)],
        compiler_params=_params(56),
        name="ffn_ctx" if is_ctx else "ffn_lat",
    )(*args)


def _pad_cols(w, width):
    return jnp.pad(w, ((0, 0), (0, width - w.shape[1])))


def _proj_weight(w_in_l):
    a_cols = 512
    b_cols = GROUP_W + SSM_CONV_CH + 8
    c_cols = 512
    wa = w_in_l[:, :a_cols]
    wb = w_in_l[:, a_cols:a_cols + b_cols]
    wc = w_in_l[:, a_cols + b_cols:a_cols + b_cols + c_cols]
    wd = w_in_l[:, a_cols + b_cols + c_cols:]
    w_zx = wb[:, :GROUP_W + SSM_CONV_CH]
    w_dt = _pad_cols(wb[:, GROUP_W + SSM_CONV_CH:], PDT_W)
    w_cq = _pad_cols(wd[:, :MLA_Q_RANK], 2 * LANES)
    w_ckv = wd[:, MLA_Q_RANK:MLA_Q_RANK + MLA_KV_RANK]
    w_kr = wd[:, MLA_Q_RANK + MLA_KV_RANK:]
    zeros = lambda n: jnp.zeros((D_MODEL, n), w_in_l.dtype)
    w_kr = jnp.concatenate([zeros(MLA_NOPE), w_kr, zeros(LANES - MLA_NOPE - MLA_ROPE)], axis=1)
    return jnp.concatenate([wa, w_zx, w_dt, wc, w_cq, w_ckv, w_kr], axis=1).astype(BF16)


def _mla_weights(w_uq_l, w_ukv_l):
    dq = MLA_NOPE + MLA_ROPE
    wq = w_uq_l.reshape(MLA_Q_RANK, 4, dq)
    wq = jnp.pad(wq, ((0, 2 * LANES - MLA_Q_RANK), (0, 0), (0, LANES - dq))).reshape(2 * LANES, 4 * LANES)
    wkv = w_ukv_l.reshape(MLA_KV_RANK, 4, MLA_NOPE + HEAD_DIM)
    wk = jnp.pad(wkv[:, :, :MLA_NOPE], ((0, 0), (0, 0), (0, LANES - MLA_NOPE))).reshape(MLA_KV_RANK, 4 * LANES)
    wv = wkv[:, :, MLA_NOPE:].reshape(MLA_KV_RANK, 4 * HEAD_DIM)
    return wq.astype(BF16), jnp.concatenate([wk, wv], axis=1).astype(BF16)


def kernel(x, c, ctx, c_ctx, norm1_w, w_mod, b_mod, w_in, attn_q_norm, attn_k_norm, ssm_conv_w, ssm_conv_b,
           ssm_dt_bias, ssm_a_log, ssm_d, ssm_norm_w, win_sink, mla_q_norm, mla_w_uq, mla_kv_norm, mla_w_ukv,
           w_out, norm2_w, ffn_w_up, ffn_conv_w, ffn_conv_b, ffn_w_down, final_norm_w):
    bsz = x.shape[0]
    depth = w_in.shape[0]
    assert x.shape[1:] == (SEQ, D_MODEL) and ctx.shape[1:] == (CTX_LEN, D_MODEL)

    tables = _axial_tables(HEAD_DIM, 0, 2) + _axial_tables(MLA_ROPE, MLA_NOPE, 1)

    n_rows = ((bsz + 1 + 7) // 8) * 8
    cvec = jnp.concatenate([c, c_ctx[None, :], jnp.zeros((n_rows - bsz - 1, D_MODEL), F32)], axis=0)
    mods_all = _mod_call(cvec, w_mod, b_mod.reshape(depth, 1, 6 * D_MODEL))

    x_lat, x_ctx = x, ctx
    for l in range(depth):
        with_ctx = l < depth - 1
        m_lat = mods_all[l, :bsz].reshape(bsz, 6, D_MODEL)
        m_ctx = jnp.broadcast_to(mods_all[l, bsz].reshape(1, 6, D_MODEL), (bsz, 6, D_MODEL))
        mods = jnp.pad(jnp.stack([m_lat, m_ctx], axis=1), ((0, 0), (0, 0), (0, 2), (0, 0)))

        wq, wkv = _mla_weights(mla_w_uq[l], mla_w_ukv[l])
        qa, ka, va, pb, pdt, qc, kc, vc, qd, kd, vd = _proj_call(
            x_lat, x_ctx, norm1_w[l].reshape(1, D_MODEL), mods, _proj_weight(w_in[l]), tables,
            jnp.tile(attn_q_norm[l], 2).reshape(1, LANES), jnp.tile(attn_k_norm[l], 2).reshape(1, LANES),
            _pad_cols(mla_q_norm[l].reshape(1, MLA_Q_RANK), 2 * LANES), mla_kv_norm[l].reshape(1, MLA_KV_RANK),
            wq, wkv)

        ya = _dense_attn_call(qa, ka, va, None, (0, 0, 0, 0), with_ctx, "attn_gqa")
        ym = _dense_attn_call(qd, kd, vd, None, (0, 1, 2, 3), with_ctx, "attn_mla")
        yw = _win_attn_call(qc, kc, vc, win_sink[l], with_ctx)
        yb = _ssd_call(
            pb, pdt, ssm_conv_w[l].T, ssm_conv_b[l].reshape(1, SSM_CONV_CH),
            jnp.tile(jnp.broadcast_to(ssm_dt_bias[l].reshape(8, 1), (8, SSD_CHUNK)), (TT // SSD_CHUNK, 1)),
            jnp.tile(jnp.broadcast_to(ssm_a_log[l].reshape(8, 1), (8, SSD_CHUNK)), (TT // SSD_CHUNK, 1)),
            jnp.repeat(ssm_d[l], HEAD_DIM).reshape(1, GROUP_W), ssm_norm_w[l].reshape(1, GROUP_W))

        ys = (ya, yb, yw, ym)
        w_o = w_out[l].astype(BF16)
        wa_up = ffn_w_up[l][:, :D_FF].astype(BF16)
        wg_up = ffn_w_up[l][:, D_FF:].astype(BF16)
        cw = ffn_conv_w[l].T
        cb = ffn_conv_b[l].reshape(1, D_FF)
        wd = ffn_w_down[l].astype(BF16)
        n2 = norm2_w[l].reshape(1, D_MODEL)
        fw = final_norm_w.reshape(1, D_MODEL)
        last = l == depth - 1

        x_lat_next = _ffn_call(ys, x_lat, mods, w_o, n2, wa_up, wg_up, cw, cb, wd, fw, 512, False, last)
        if with_ctx:
            x_ctx = _ffn_call(ys, x_ctx, mods, w_o, n2, wa_up, wg_up, cw, cb, wd, fw, CTX_LEN, True, False)
        x_lat = x_lat_next
    return x_lat
```

```python
import functools

import numpy as np
import jax
import jax.numpy as jnp
from jax import lax
from jax.experimental import pallas as pl
from jax.experimental.pallas import tpu as pltpu

F32 = jnp.float32
BF16 = jnp.bfloat16

D_MODEL = 1024
SEQ = 2048
CTX_LEN = 256
TT = SEQ + CTX_LEN
GRID_W = 64
ROPE_THETA = 10000.0
NORM_EPS = 1e-6
HEAD_DIM = 64
GROUP_W = 256
WINDOW = 128
SSM_STATE = 128
SSM_CONV_CH = 768
MLA_NOPE = 64
MLA_ROPE = 32
MLA_Q_RANK = 192
MLA_KV_RANK = 128
D_FF = 2816

LANES = 128
BF16_ROWS = 16
TOK_TILE = 256
N_LAT_TILES = SEQ // TOK_TILE
N_TILES = TT // TOK_TILE
SSD_CHUNK = 128
FFN_TILE = 1024
FFN_CHUNK = 256
FFN_HALO = 16
LOG2E = 1.4426950408889634

PA_W, PB_W, PDT_W, PC_W, PD_W = 512, 1024, 128, 512, 512
OFF_A = 0
OFF_B = OFF_A + PA_W
OFF_DT = OFF_B + PB_W
OFF_C = OFF_DT + PDT_W
OFF_D = OFF_C + PC_W
PROJ_W = OFF_D + PD_W


def _resident(shape, index=None):
    index = (0,) * len(shape) if index is None else tuple(index)
    return pl.BlockSpec(shape, lambda *_: index, pipeline_mode=pl.Buffered(1))


def _params(vmem_mb, ndims=2):
    return pltpu.CompilerParams(
        dimension_semantics=("arbitrary",) * ndims,
        vmem_limit_bytes=vmem_mb * 1024 * 1024,
    )


def _sigmoid(v):
    return 1.0 / (1.0 + jnp.exp(-v))


def _dot(a, b):
    return jnp.dot(a, b, preferred_element_type=F32)


def _dot_nt(a, b):
    return lax.dot_general(a, b, (((1,), (1,)), ((), ())), preferred_element_type=F32)


def _split3(v):
    t1 = v.astype(BF16)
    r1 = v - t1.astype(F32)
    t2 = r1.astype(BF16)
    t3 = (r1 - t2.astype(F32)).astype(BF16)
    return t1, t2, t3


def _mod_kernel(c_ref, w_ref, b_ref, o_ref):
    a = c_ref[...]
    s = a * _sigmoid(a)
    w = w_ref[...]
    s_hi = s.astype(BF16)
    s_lo = (s - s_hi.astype(F32)).astype(BF16)
    w_hi = w.astype(BF16)
    w_lo = (w - w_hi.astype(F32)).astype(BF16)
    o_ref[...] = _dot(s_hi, w_hi) + _dot(s_hi, w_lo) + _dot(s_lo, w_hi) + b_ref[...]


def _mod_call(cvec, w_mod, b_mod):
    n_layers, d, n6 = w_mod.shape
    rows = cvec.shape[0]
    tn = 1024
    return pl.pallas_call(
        _mod_kernel,
        grid=(n_layers, n6 // tn),
        in_specs=[
            pl.BlockSpec((rows, d), lambda l, j: (0, 0)),
            pl.BlockSpec((None, d, tn), lambda l, j: (l, 0, j)),
            pl.BlockSpec((None, 1, tn), lambda l, j: (l, 0, j)),
        ],
        out_specs=pl.BlockSpec((None, rows, tn), lambda l, j: (l, 0, j)),
        out_shape=jax.ShapeDtypeStruct((n_layers, rows, n6), F32),
        compiler_params=_params(40),
        name="mod",
    )(cvec, w_mod, b_mod)


def _rope(x, cos, sin_signed, half):
    lane = lax.broadcasted_iota(jnp.int32, x.shape, 1)
    first = (lane & (2 * half - 1)) < half
    partner = jnp.where(first, pltpu.roll(x, LANES - half, 1), pltpu.roll(x, half, 1))
    return x * cos + partner * sin_signed


def _axial_tables(rot_dim, lane_lo, reps):
    half = rot_dim // 2
    t = np.arange(SEQ)
    row = (t // GRID_W).astype(np.float32)
    col = (t % GRID_W).astype(np.float32)
    inv_freq = np.power(np.float32(ROPE_THETA), -np.arange(0, half, 2, dtype=np.float32) / np.float32(half))
    inv_freq = inv_freq.astype(np.float32)
    ang_r = (row[:, None] * inv_freq[None, :]).astype(np.float32)
    ang_c = (col[:, None] * inv_freq[None, :]).astype(np.float32)
    ang = np.concatenate([ang_r, ang_r, ang_c, ang_c], axis=-1).astype(np.float64)
    cos = np.cos(ang)
    sin = np.sin(ang)
    quarter = half // 2
    sign = np.where((np.arange(rot_dim) % half) < quarter, -1.0, 1.0)
    cos_t = np.ones((TT, LANES), np.float64)
    sin_t = np.zeros((TT, LANES), np.float64)
    for r in range(reps):
        lo = lane_lo + r * rot_dim
        cos_t[:SEQ, lo:lo + rot_dim] = cos
        sin_t[:SEQ, lo:lo + rot_dim] = sin * sign[None, :]
    return jnp.asarray(cos_t, F32), jnp.asarray(sin_t, F32)


def _store_value_slabs(pair, swapped, lo, v_ref):
    v_ref[0] = jnp.where(lo, pair, 1.0).astype(BF16)
    v_ref[1] = jnp.where(lo, 1.0, swapped).astype(BF16)
    v_ref[2] = jnp.where(lo, swapped, 1.0).astype(BF16)
    v_ref[3] = jnp.where(lo, 1.0, pair).astype(BF16)


def _gqa_prep(p, cos, sin, qn, kn, do_norm, q_ref, k_ref, v_ref):
    lane = lax.broadcasted_iota(jnp.int32, (p.shape[0], LANES), 1)
    lo = lane < HEAD_DIM

    def head_norm(v, gain):
        v2 = v * v
        s_lo = jnp.sum(jnp.where(lo, v2, 0.0), axis=-1, keepdims=True)
        s_hi = jnp.sum(jnp.where(lo, 0.0, v2), axis=-1, keepdims=True)
        ms = jnp.where(lo, s_lo, s_hi) * (1.0 / HEAD_DIM)
        return v * lax.rsqrt(ms + NORM_EPS) * gain

    scale = HEAD_DIM ** -0.5 * LOG2E
    for j in range(2):
        v = p[:, j * LANES:(j + 1) * LANES]
        if do_norm:
            v = head_norm(v, qn)
        v = _rope(v, cos, sin, HEAD_DIM // 4) * scale
        sw = pltpu.roll(v, HEAD_DIM, 1)
        if j == 0:
            q_ref[:, 0:LANES] = jnp.where(lo, v, 0.0).astype(BF16)
            q_ref[:, LANES:2 * LANES] = jnp.where(lo, sw, 0.0).astype(BF16)
        else:
            q_ref[:, 2 * LANES:3 * LANES] = jnp.where(lo, 0.0, sw).astype(BF16)
            q_ref[:, 3 * LANES:4 * LANES] = jnp.where(lo, 0.0, v).astype(BF16)

    k = p[:, 2 * LANES:3 * LANES]
    if do_norm:
        k = head_norm(k, kn)
    k_ref[0] = _rope(k, cos, sin, HEAD_DIM // 4).astype(BF16)

    vv = p[:, 3 * LANES:4 * LANES]
    _store_value_slabs(vv, pltpu.roll(vv, HEAD_DIM, 1), lo, v_ref)


def _mla_prep(p, cos, sin, gq, gkv, wq_ref, wkv_ref, q_ref, k_ref, v_ref):
    half = MLA_ROPE // 4
    cq = p[:, 0:2 * LANES]
    ms = jnp.sum(cq * cq, axis=-1, keepdims=True) * (1.0 / MLA_Q_RANK)
    cqn = (cq * lax.rsqrt(ms + NORM_EPS) * gq).astype(BF16)
    q = _dot(cqn, wq_ref[...])
    scale = (MLA_NOPE + MLA_ROPE) ** -0.5 * LOG2E
    for h in range(4):
        qh = _rope(q[:, h * LANES:(h + 1) * LANES], cos, sin, half) * scale
        q_ref[:, h * LANES:(h + 1) * LANES] = qh.astype(BF16)

    ckv = p[:, 2 * LANES:3 * LANES]
    ms = jnp.mean(ckv * ckv, axis=-1, keepdims=True)
    ckvn = (ckv * lax.rsqrt(ms + NORM_EPS) * gkv).astype(BF16)
    kv = _dot(ckvn, wkv_ref[...])
    kr = _rope(p[:, 3 * LANES:4 * LANES], cos, sin, half)
    for h in range(4):
        k_ref[h] = (kv[:, h * LANES:(h + 1) * LANES] + kr).astype(BF16)
    lane = lax.broadcasted_iota(jnp.int32, (p.shape[0], LANES), 1)
    lo = lane < HEAD_DIM
    v01 = kv[:, 4 * LANES:5 * LANES]
    v23 = kv[:, 5 * LANES:6 * LANES]
    v_ref[0] = jnp.where(lo, v01, 1.0).astype(BF16)
    v_ref[1] = jnp.where(lo, 1.0, v01).astype(BF16)
    v_ref[2] = jnp.where(lo, v23, 1.0).astype(BF16)
    v_ref[3] = jnp.where(lo, 1.0, v23).astype(BF16)


def _proj_kernel(xl_ref, xc_ref, nw_ref, m_ref, w_ref, cosa_ref, sina_ref, cosm_ref, sinm_ref,
                 qn_ref, kn_ref, gq_ref, gkv_ref, wq_ref, wkv_ref,
                 qa_ref, ka_ref, va_ref, pb_ref, pdt_ref, qc_ref, kc_ref, vc_ref, qd_ref, kd_ref, vd_ref):
    i = pl.program_id(1)
    x = jnp.where(i < N_LAT_TILES, xl_ref[...], xc_ref[...])
    ms = jnp.mean(x * x, axis=-1, keepdims=True)
    y = x * lax.rsqrt(ms + NORM_EPS) * nw_ref[...]
    h = (y * (1.0 + m_ref[1:2, :]) + m_ref[0:1, :]).astype(BF16)
    cosa = cosa_ref[...]
    sina = sina_ref[...]
    _mla_prep(_dot(h, w_ref[:, OFF_D:OFF_D + PD_W]), cosm_ref[...], sinm_ref[...], gq_ref[...], gkv_ref[...],
              wq_ref, wkv_ref, qd_ref, kd_ref, vd_ref)
    _gqa_prep(_dot(h, w_ref[:, OFF_A:OFF_A + PA_W]), cosa, sina, qn_ref[...], kn_ref[...], True,
              qa_ref, ka_ref, va_ref)
    _gqa_prep(_dot(h, w_ref[:, OFF_C:OFF_C + PC_W]), cosa, sina, None, None, False,
              qc_ref, kc_ref, vc_ref)
    pdt_ref[...] = _dot(h, w_ref[:, OFF_DT:OFF_DT + PDT_W])
    pb_ref[...] = _dot(h, w_ref[:, OFF_B:OFF_B + PB_W]).astype(BF16)


def _proj_call(x_lat, x_ctx, norm_w, mods, layer, w, tables, qn, kn, gq, gkv, wq, wkv):
    b = x_lat.shape[0]
    tok = lambda width: pl.BlockSpec((None, TOK_TILE, width), lambda bi, i: (bi, i, 0))
    slab = lambda n: pl.BlockSpec((None, n, TOK_TILE, LANES), lambda bi, i: (bi, 0, i, 0))
    table = pl.BlockSpec((TOK_TILE, LANES), lambda bi, i: (i, 0))
    q_shape = jax.ShapeDtypeStruct((b, TT, 4 * LANES), BF16)
    slab_shape = lambda n: jax.ShapeDtypeStruct((b, n, TT, LANES), BF16)
    return pl.pallas_call(
        _proj_kernel,
        grid=(b, N_TILES),
        in_specs=[
            pl.BlockSpec((None, TOK_TILE, D_MODEL), lambda bi, i: (bi, jnp.minimum(i, N_LAT_TILES - 1), 0)),
            pl.BlockSpec((None, TOK_TILE, D_MODEL), lambda bi, i: (bi, 0, 0)),
            _resident((1, D_MODEL)),
            pl.BlockSpec((None, None, 8, D_MODEL), lambda bi, i: (bi, i // N_LAT_TILES, 0, 0)),
            _resident((None, D_MODEL, PROJ_W), (layer, 0, 0)),
            table, table, table, table,
            _resident((1, LANES)), _resident((1, LANES)),
            _resident((1, 2 * LANES)), _resident((1, LANES)),
            _resident((2 * LANES, 4 * LANES)), _resident((LANES, 6 * LANES)),
        ],
        out_specs=[tok(4 * LANES), slab(1), slab(4), tok(PB_W), tok(PDT_W),
                   tok(4 * LANES), slab(1), slab(4), tok(4 * LANES), slab(4), slab(4)],
        out_shape=[q_shape, slab_shape(1), slab_shape(4),
                   jax.ShapeDtypeStruct((b, TT, PB_W), BF16), jax.ShapeDtypeStruct((b, TT, PDT_W), F32),
                   q_shape, slab_shape(1), slab_shape(4), q_shape, slab_shape(4), slab_shape(4)],
        compiler_params=_params(48),
        name="proj",
    )(x_lat, x_ctx, norm_w, mods, w, *tables, qn, kn, gq, gkv, wq, wkv)


def _softmax_pv(s_parts, v_parts, sink):
    m = s_parts[0].max(axis=-1, keepdims=True)
    for s in s_parts[1:]:
        m = jnp.maximum(m, s.max(axis=-1, keepdims=True))
    if sink is not None:
        m = jnp.maximum(m, sink)
    acc = None
    for s, v in zip(s_parts, v_parts):
        o = _dot(jnp.exp2(s - m).astype(BF16), v)
        acc = o if acc is None else acc + o
    den = pltpu.roll(acc, HEAD_DIM, 1)
    if sink is not None:
        den = den + jnp.exp2(sink - m)
    return acc / den


def _merge_heads(outs, store):
    lane = lax.broadcasted_iota(jnp.int32, outs[0].shape, 1)
    lo = lane < HEAD_DIM
    store(0, jnp.where(lo, outs[0], outs[1]).astype(BF16))
    store(1, jnp.where(lo, outs[2], outs[3]).astype(BF16))


ATT_TILE = 256
DENSE_TILE = 512


def _dense_attn_kernel(*refs, k_of, has_sink, with_ctx):
    if has_sink:
        sink_ref, q_ref, k_ref, v_ref, o_ref = refs
    else:
        q_ref, k_ref, v_ref, o_ref = refs
        sink_ref = None

    def tile(r0, rows, k_lo, k_hi):
        def scores(h):
            return _dot_nt(q_ref[pl.ds(r0, rows), h * LANES:(h + 1) * LANES], k_ref[k_of[h], k_lo:k_hi, :])

        def store(j, val):
            o_ref[pl.ds(r0, rows), j * LANES:(j + 1) * LANES] = val

        outs = []
        s_next = scores(0)
        for h in range(4):
            s = s_next
            if h < 3:
                s_next = scores(h + 1)
            sink = sink_ref[h] * LOG2E if has_sink else None
            outs.append(_softmax_pv([s], [v_ref[h, k_lo:k_hi, :]], sink))
        _merge_heads(outs, store)

    def body(t, carry):
        tile(pl.multiple_of(t * DENSE_TILE, DENSE_TILE), DENSE_TILE, 0, TT)
        return carry

    lax.fori_loop(0, SEQ // DENSE_TILE, body, 0)
    if with_ctx:
        tile(SEQ, CTX_LEN, SEQ, TT)
    else:
        o_ref[SEQ:TT, :] = jnp.zeros((CTX_LEN, GROUP_W), BF16)


def _dense_attn_call(q, k, v, sink, k_of, with_ctx, name):
    b = q.shape[0]
    n_k = k.shape[1]
    has_sink = sink is not None
    in_specs = [
        pl.BlockSpec((None, TT, 4 * LANES), lambda bi: (bi, 0, 0)),
        pl.BlockSpec((None, n_k, TT, LANES), lambda bi: (bi, 0, 0, 0)),
        pl.BlockSpec((None, 4, TT, LANES), lambda bi: (bi, 0, 0, 0)),
    ]
    args = [q, k, v]
    if has_sink:
        in_specs = [pl.BlockSpec(memory_space=pltpu.SMEM)] + in_specs
        args = [sink] + args
    return pl.pallas_call(
        functools.partial(_dense_attn_kernel, k_of=k_of, has_sink=has_sink, with_ctx=with_ctx),
        grid=(b,),
        in_specs=in_specs,
        out_specs=pl.BlockSpec((None, TT, GROUP_W), lambda bi: (bi, 0, 0)),
        out_shape=jax.ShapeDtypeStruct((b, TT, GROUP_W), BF16),
        compiler_params=_params(48, 1),
        name=name,
    )(*args)


WIN_BAND = ATT_TILE + 2 * WINDOW


def _win_attn_kernel(sink_ref, q_ref, k_ref, v_ref, o_ref, *, with_ctx):
    k_ctx = k_ref[0, SEQ:TT, :]

    def finish(r0, rows, outs):
        def store(j, val):
            o_ref[pl.ds(r0, rows), j * LANES:(j + 1) * LANES] = val
        _merge_heads(outs, store)

    def band_tile(t, carry):
        r0 = pl.multiple_of(t * ATT_TILE, ATT_TILE)
        start = pl.multiple_of(jnp.clip(r0 - WINDOW, 0, SEQ - WIN_BAND), WINDOW)
        qpos = r0 + lax.broadcasted_iota(jnp.int32, (ATT_TILE, WIN_BAND), 0)
        kpos = start + lax.broadcasted_iota(jnp.int32, (ATT_TILE, WIN_BAND), 1)
        mask = jnp.abs(kpos - qpos) <= WINDOW
        k_band = k_ref[0, pl.ds(start, WIN_BAND), :]

        def scores(h):
            qh = q_ref[pl.ds(r0, ATT_TILE), h * LANES:(h + 1) * LANES]
            return [jnp.where(mask, _dot_nt(qh, k_band), -jnp.inf), _dot_nt(qh, k_ctx)]

        outs = []
        s_next = scores(0)
        for h in range(4):
            s_parts = s_next
            if h < 3:
                s_next = scores(h + 1)
            outs.append(_softmax_pv(
                s_parts,
                [v_ref[h, pl.ds(start, WIN_BAND), :], v_ref[h, SEQ:TT, :]],
                sink_ref[h] * LOG2E))
        finish(r0, ATT_TILE, outs)
        return carry

    lax.fori_loop(0, SEQ // ATT_TILE, band_tile, 0)
    if with_ctx:
        outs = []
        for h in range(4):
            qh = q_ref[SEQ:TT, h * LANES:(h + 1) * LANES]
            outs.append(_softmax_pv([_dot_nt(qh, k_ctx)], [v_ref[h, SEQ:TT, :]], sink_ref[h] * LOG2E))
        finish(SEQ, CTX_LEN, outs)
    else:
        o_ref[SEQ:TT, :] = jnp.zeros((CTX_LEN, GROUP_W), BF16)


def _win_attn_call(q, k, v, sink, with_ctx):
    b = q.shape[0]
    return pl.pallas_call(
        functools.partial(_win_attn_kernel, with_ctx=with_ctx),
        grid=(b,),
        in_specs=[
            pl.BlockSpec(memory_space=pltpu.SMEM),
            pl.BlockSpec((None, TT, 4 * LANES), lambda bi: (bi, 0, 0)),
            pl.BlockSpec((None, 1, TT, LANES), lambda bi: (bi, 0, 0, 0)),
            pl.BlockSpec((None, 4, TT, LANES), lambda bi: (bi, 0, 0, 0)),
        ],
        out_specs=pl.BlockSpec((None, TT, GROUP_W), lambda bi: (bi, 0, 0)),
        out_shape=jax.ShapeDtypeStruct((b, TT, GROUP_W), BF16),
        compiler_params=_params(32, 1),
        name="attn_win",
    )(sink, q, k, v)


def _ssd_kernel(pb_ref, dt_ref, cw_ref, cb_ref, dtb_ref, alog_ref, dsk_ref, nw_ref, y_ref,
                xbc_s, st_s, yd_s, csc_s, dt_s, cs_s, w_s, ecs_s, etot_s, h_s):
    lc = SSD_CHUNK
    n_chunks = TT // lc
    n_lat = SEQ // lc
    n_ctx = CTX_LEN // lc
    row_i = lax.broadcasted_iota(jnp.int32, (lc, lc), 0)
    col_i = lax.broadcasted_iota(jnp.int32, (lc, lc), 1)
    top = row_i < HEAD_DIM
    keep_f = row_i <= col_i
    keep_b = row_i >= col_i

    def pair_rows(tab, j0):
        return jnp.where(top, tab[j0:j0 + 1, :], tab[j0 + 1:j0 + 2, :])

    pre = jnp.concatenate([dt_ref[c * lc:(c + 1) * lc, :].T[0:8, :] for c in range(n_chunks)], axis=0) + dtb_ref[...]
    dt_all = jnp.maximum(pre, 0.0) + jnp.log1p(jnp.exp(-jnp.abs(pre)))
    da = dt_all * (-jnp.exp(alog_ref[...]))
    lane_t = lax.broadcasted_iota(jnp.int32, da.shape, 1)
    row_t = lax.broadcasted_iota(jnp.int32, da.shape, 0)
    cs = da
    k = 1
    while k < lc:
        cs = cs + jnp.where(lane_t >= k, pltpu.roll(cs, k, 1), 0.0)
        k *= 2
    tot = cs[:, lc - 1:lc]
    cs_all = jnp.where((row_t & 7) < 4, cs, tot - cs + da)
    dt_s[...] = dt_all
    cs_s[...] = cs_all
    w_s[...] = jnp.exp(tot - cs_all)
    ecs_s[...] = jnp.exp(cs_all)
    etot_s[...] = jnp.broadcast_to(jnp.exp(tot), da.shape)
    for c in range(n_chunks):
        csc_s[c] = jnp.concatenate([cs_all[c * 8:(c + 1) * 8, :]] * (lc // 8), axis=0).T

    def chunk_a(c):
        r0 = pl.multiple_of(c * lc, lc)
        t0 = pl.multiple_of(c * 8, 8)
        first = jnp.logical_or(c == 0, c == n_lat)
        last = jnp.logical_or(c == n_lat - 1, c == n_chunks - 1)
        u = pb_ref[pl.ds(r0, lc), GROUP_W:GROUP_W + SSM_CONV_CH].astype(F32)
        rp = pl.multiple_of(jnp.maximum(r0 - BF16_ROWS, 0), BF16_ROWS)
        rn = pl.multiple_of(jnp.minimum(r0 + lc, TT - BF16_ROWS), BF16_ROWS)
        prev_blk = pb_ref[pl.ds(rp, BF16_ROWS), GROUP_W:GROUP_W + SSM_CONV_CH].astype(F32)
        next_blk = pb_ref[pl.ds(rn, BF16_ROWS), GROUP_W:GROUP_W + SSM_CONV_CH].astype(F32)
        prev_row = jnp.where(first, 0.0, prev_blk[BF16_ROWS - 1:BF16_ROWS, :])
        next_row = jnp.where(last, 0.0, next_blk[0:1, :])
        rows = lax.broadcasted_iota(jnp.int32, (lc, SSM_CONV_CH), 0)
        up = jnp.where(rows == 0, prev_row, pltpu.roll(u, 1, 0))
        un = jnp.where(rows == lc - 1, next_row, pltpu.roll(u, lc - 1, 0))
        v = cw_ref[0:1, :] * up + cw_ref[1:2, :] * u + cw_ref[2:3, :] * un + cb_ref[...]
        act = v * _sigmoid(v)
        xbc_s[pl.ds(r0, lc), :] = act
        xs = act[:, 0:GROUP_W]
        bm = act[:, GROUP_W:2 * GROUP_W]
        cm = act[:, 2 * GROUP_W:3 * GROUP_W]
        dt_t = dt_s[pl.ds(t0, 8), :]
        cs_t = cs_s[pl.ds(t0, 8), :]
        w_t = w_s[pl.ds(t0, 8), :]
        cs_cols = csc_s[c]

        for g in range(2):
            bm_g = bm[:, g * LANES:(g + 1) * LANES].astype(BF16)
            cm_g = cm[:, g * LANES:(g + 1) * LANES].astype(BF16)
            cb_t = _dot_nt(bm_g, cm_g)
            xs_t = xs[:, g * LANES:(g + 1) * LANES].T
            yd = None
            for d in range(2):
                j0 = d * 4 + 2 * g
                keep = keep_f if d == 0 else keep_b
                xdt_t = xs_t * pair_rows(dt_t, j0)
                xdt_b = xdt_t.astype(BF16)
                parts = []
                for hh in range(2):
                    j = j0 + hh
                    seg_t = cs_t[j:j + 1, :] - cs_cols[:, j:j + 1]
                    dec_t = jnp.exp(jnp.where(keep, seg_t, -jnp.inf))
                    parts.append(_dot(xdt_b[hh * HEAD_DIM:(hh + 1) * HEAD_DIM, :], (cb_t * dec_t).astype(BF16)))
                y_dg = jnp.concatenate(parts, axis=0)
                yd = y_dg if yd is None else yd + y_dg
                xw_t = xdt_t * pair_rows(w_t, j0)
                st_s[c * 4 + d * 2 + g] = _dot(xw_t.astype(BF16), bm_g)
            yd_s[c * 2 + g] = yd

    def phase_a(t, carry):
        for u in range(3):
            chunk_a(3 * t + u)
        return carry

    lax.fori_loop(0, n_chunks // 3, phase_a, 0)

    h_s[...] = jnp.zeros_like(h_s)

    def phase_b(base_chunk, n_seg):
        def body(t, carry):
            for d in range(2):
                c = base_chunk + t if d == 0 else base_chunk + n_seg - 1 - t
                e = etot_s[pl.ds(pl.multiple_of(c * 8, 8), 8), :]
                for g in range(2):
                    slot = c * 4 + d * 2 + g
                    s_c = st_s[slot]
                    h_in = h_s[d * 2 + g]
                    st_s[slot] = h_in
                    h_s[d * 2 + g] = h_in * pair_rows(e, d * 4 + 2 * g) + s_c
            return carry
        lax.fori_loop(0, n_seg, body, 0)

    phase_b(n_lat, n_ctx)
    phase_b(0, n_lat)

    def chunk_c(c):
        r0 = pl.multiple_of(c * lc, lc)
        xs = xbc_s[pl.ds(r0, lc), 0:GROUP_W]
        cm = xbc_s[pl.ds(r0, lc), 2 * GROUP_W:3 * GROUP_W]
        ecs = ecs_s[pl.ds(pl.multiple_of(c * 8, 8), 8), :]
        halves = []
        for g in range(2):
            cm_g = cm[:, g * LANES:(g + 1) * LANES].astype(BF16)
            y_t = yd_s[c * 2 + g]
            for d in range(2):
                h_in = st_s[c * 4 + d * 2 + g].astype(BF16)
                y_t = y_t + _dot_nt(h_in, cm_g) * pair_rows(ecs, d * 4 + 2 * g)
            halves.append(y_t.T)
        y = jnp.concatenate(halves, axis=1) + xs * dsk_ref[...]
        z = pb_ref[pl.ds(r0, lc), 0:GROUP_W].astype(F32)
        gte = y * (z * _sigmoid(z))
        ms = jnp.mean(gte * gte, axis=-1, keepdims=True)
        y_ref[pl.ds(r0, lc), :] = (gte * lax.rsqrt(ms + NORM_EPS) * nw_ref[...]).astype(BF16)

    def phase_c(t, carry):
        for u in range(6):
            chunk_c(6 * t + u)
        return carry

    lax.fori_loop(0, n_chunks // 6, phase_c, 0)


def _ssd_call(pb, pdt, cw, cb, dtb, alog, dsk, nw):
    b = pb.shape[0]
    n_chunks = TT // SSD_CHUNK
    return pl.pallas_call(
        _ssd_kernel,
        grid=(b,),
        in_specs=[
            pl.BlockSpec((None, TT, PB_W), lambda bi: (bi, 0, 0)),
            pl.BlockSpec((None, TT, PDT_W), lambda bi: (bi, 0, 0)),
            _resident((3, SSM_CONV_CH)),
            _resident((1, SSM_CONV_CH)),
            _resident((n_chunks * 8, SSD_CHUNK)),
            _resident((n_chunks * 8, SSD_CHUNK)),
            _resident((1, GROUP_W)),
            _resident((1, GROUP_W)),
        ],
        out_specs=pl.BlockSpec((None, TT, GROUP_W), lambda bi: (bi, 0, 0)),
        out_shape=jax.ShapeDtypeStruct((b, TT, GROUP_W), BF16),
        scratch_shapes=[
            pltpu.VMEM((TT, SSM_CONV_CH), F32),
            pltpu.VMEM((n_chunks * 4, LANES, SSM_STATE), F32),
            pltpu.VMEM((n_chunks * 2, LANES, SSD_CHUNK), F32),
            pltpu.VMEM((n_chunks, SSD_CHUNK, LANES), F32),
            pltpu.VMEM((n_chunks * 8, SSD_CHUNK), F32),
            pltpu.VMEM((n_chunks * 8, SSD_CHUNK), F32),
            pltpu.VMEM((n_chunks * 8, SSD_CHUNK), F32),
            pltpu.VMEM((n_chunks * 8, SSD_CHUNK), F32),
            pltpu.VMEM((n_chunks * 8, SSD_CHUNK), F32),
            pltpu.VMEM((4, LANES, SSM_STATE), F32),
        ],
        compiler_params=_params(48, 1),
        name="ssd",
    )(pb, pdt, cw, cb, dtb, alog, dsk, nw)


def _ffn_kernel(*refs, tm, final_norm):
    y_refs = refs[0:12]
    x_ref, xp_ref, xn_ref, m_ref, wo_ref, nw_ref, wa_ref, wg_ref, cw_ref, cb_ref, wd_ref, fw_ref = refs[12:24]
    o_ref, u_s = refs[24:26]
    i = pl.program_id(1)
    n_i = pl.num_programs(1)
    hl = FFN_HALO
    acc = None
    for j in range(4):
        ycat = jnp.concatenate([r[...] for r in y_refs[3 * j:3 * j + 3]], axis=0)
        d = _dot(ycat, wo_ref[j * GROUP_W:(j + 1) * GROUP_W, :])
        acc = d if acc is None else acc + d
    x1 = jnp.concatenate([x_ref[...], xp_ref[...], xn_ref[...]], axis=0) + m_ref[2:3, :] * acc
    ms = jnp.mean(x1 * x1, axis=-1, keepdims=True)
    y = x1 * lax.rsqrt(ms + NORM_EPS) * nw_ref[...]
    h = (y * (1.0 + m_ref[4:5, :]) + m_ref[3:4, :]).astype(BF16)
    h_mid = h[0:tm, :]
    rid = lax.broadcasted_iota(jnp.int32, (tm, FFN_CHUNK), 0)
    has_prev = i > 0
    has_next = i < n_i - 1

    for j in range(D_FF // FFN_CHUNK):
        c0 = j * FFN_CHUNK
        a = _dot(h_mid, wa_ref[:, pl.ds(c0, FFN_CHUNK)])
        g = _dot(h, wg_ref[:, pl.ds(c0, FFN_CHUNK)])
        g_mid = g[0:tm, :]
        row_before = jnp.where(has_prev, g[tm + hl - 1:tm + hl, :], 0.0)
        row_after = jnp.where(has_next, g[tm + hl:tm + hl + 1, :], 0.0)
        g_prev = jnp.where(rid == 0, row_before, pltpu.roll(g_mid, 1, 0))
        g_next = jnp.where(rid == tm - 1, row_after, pltpu.roll(g_mid, tm - 1, 0))
        cw = cw_ref[:, pl.ds(c0, FFN_CHUNK)]
        gc = cw[0:1, :] * g_prev + cw[1:2, :] * g_mid + cw[2:3, :] * g_next + cb_ref[:, pl.ds(c0, FFN_CHUNK)]
        u_s[:, pl.ds(c0, FFN_CHUNK)] = (a * (gc * _sigmoid(gc))).astype(BF16)

    out = x1[0:tm, :] + m_ref[5:6, :] * _dot(u_s[...], wd_ref[...])
    if final_norm:
        ms2 = jnp.mean(out * out, axis=-1, keepdims=True)
        out = out * lax.rsqrt(ms2 + NORM_EPS) * fw_ref[...]
    o_ref[...] = out


def _ffn_call(ys, x, mods, layer, wo, norm_w, w_up, cw, cb, wd, fw, tm, is_ctx, final_norm):
    b, t, _ = x.shape
    n_tiles = t // tm
    hl = FFN_HALO
    hb = tm // hl
    row0 = SEQ if is_ctx else 0
    mi = 1 if is_ctx else 0

    def triple(width, rows, base):
        off_m = base // tm
        off_h = base // hl
        last = rows // hl - 1
        return [
            pl.BlockSpec((None, tm, width), lambda bi, i: (bi, i + off_m, 0)),
            pl.BlockSpec((None, hl, width), lambda bi, i: (bi, jnp.maximum(i * hb + off_h - 1, 0), 0)),
            pl.BlockSpec((None, hl, width), lambda bi, i: (bi, jnp.minimum((i + 1) * hb + off_h, last), 0)),
        ]

    in_specs = []
    args = []
    for yv in ys:
        in_specs += triple(GROUP_W, TT, row0)
        args += [yv, yv, yv]
    in_specs += triple(D_MODEL, t, 0)
    args += [x, x, x]
    in_specs += [
        pl.BlockSpec((None, None, 8, D_MODEL), lambda bi, i: (bi, mi, 0, 0)),
        _resident((None, 4 * GROUP_W, D_MODEL), (layer, 0, 0)),
        _resident((1, D_MODEL)),
        _resident((None, D_MODEL, D_FF), (layer, 0, 0)),
        _resident((None, D_MODEL, D_FF), (layer, 0, 1)),
        _resident((None, 3, D_FF), (layer, 0, 0)),
        _resident((None, 1, D_FF), (layer, 0, 0)),
        _resident((None, D_FF, D_MODEL), (layer, 0, 0)),
        _resident((1, D_MODEL)),
    ]
    args += [mods, wo, norm_w, w_up, w_up, cw, cb, wd, fw]
    return pl.pallas_call(
        functools.partial(_ffn_kernel, tm=tm, final_norm=final_norm),
        grid=(b, n_tiles),
        in_specs=in_specs,
        out_specs=pl.BlockSpec((None, tm, D_MODEL), lambda bi, i: (bi, i, 0)),
        out_shape=jax.ShapeDtypeStruct((b, t, D_MODEL), F32),
        scratch_shapes=[pltpu.VMEM((tm, D_FF), BF16)],
        compiler_params=_params(60),
        name="ffn_ctx" if is_ctx else "ffn_lat",
    )(*args)


def _pad_cols(w, width):
    return jnp.pad(w, ((0, 0), (0, width - w.shape[1])))


def _proj_weight(w_in):
    a_cols = 512
    b_cols = GROUP_W + SSM_CONV_CH + 8
    c_cols = 512
    wa = w_in[..., :a_cols]
    wb = w_in[..., a_cols:a_cols + b_cols]
    wc = w_in[..., a_cols + b_cols:a_cols + b_cols + c_cols]
    wd = w_in[..., a_cols + b_cols + c_cols:]
    zeros = lambda n: jnp.zeros(w_in.shape[:-1] + (n,), w_in.dtype)
    w_zx = wb[..., :GROUP_W + SSM_CONV_CH]
    w_dt = jnp.concatenate([wb[..., GROUP_W + SSM_CONV_CH:], zeros(PDT_W - 8)], axis=-1)
    w_cq = jnp.concatenate([wd[..., :MLA_Q_RANK], zeros(2 * LANES - MLA_Q_RANK)], axis=-1)
    w_ckv = wd[..., MLA_Q_RANK:MLA_Q_RANK + MLA_KV_RANK]
    w_kr = jnp.concatenate([zeros(MLA_NOPE), wd[..., MLA_Q_RANK + MLA_KV_RANK:],
                            zeros(LANES - MLA_NOPE - MLA_ROPE)], axis=-1)
    return jnp.concatenate([wa, w_zx, w_dt, wc, w_cq, w_ckv, w_kr], axis=-1).astype(BF16)


def _mla_weights(w_uq_l, w_ukv_l):
    dq = MLA_NOPE + MLA_ROPE
    wq = w_uq_l.reshape(MLA_Q_RANK, 4, dq)
    wq = jnp.pad(wq, ((0, 2 * LANES - MLA_Q_RANK), (0, 0), (0, LANES - dq))).reshape(2 * LANES, 4 * LANES)
    wkv = w_ukv_l.reshape(MLA_KV_RANK, 4, MLA_NOPE + HEAD_DIM)
    wk = jnp.pad(wkv[:, :, :MLA_NOPE], ((0, 0), (0, 0), (0, LANES - MLA_NOPE))).reshape(MLA_KV_RANK, 4 * LANES)
    wv = wkv[:, :, MLA_NOPE:].reshape(MLA_KV_RANK, 4 * HEAD_DIM)
    return wq.astype(BF16), jnp.concatenate([wk, wv], axis=1).astype(BF16)


def kernel(x, c, ctx, c_ctx, norm1_w, w_mod, b_mod, w_in, attn_q_norm, attn_k_norm, ssm_conv_w, ssm_conv_b,
           ssm_dt_bias, ssm_a_log, ssm_d, ssm_norm_w, win_sink, mla_q_norm, mla_w_uq, mla_kv_norm, mla_w_ukv,
           w_out, norm2_w, ffn_w_up, ffn_conv_w, ffn_conv_b, ffn_w_down, final_norm_w):
    bsz = x.shape[0]
    depth = w_in.shape[0]
    assert x.shape[1:] == (SEQ, D_MODEL) and ctx.shape[1:] == (CTX_LEN, D_MODEL)

    tables = _axial_tables(HEAD_DIM, 0, 2) + _axial_tables(MLA_ROPE, MLA_NOPE, 1)

    n_rows = ((bsz + 1 + 7) // 8) * 8
    cvec = jnp.concatenate([c, c_ctx[None, :], jnp.zeros((n_rows - bsz - 1, D_MODEL), F32)], axis=0)
    mods_all = _mod_call(cvec, w_mod, b_mod.reshape(depth, 1, 6 * D_MODEL))

    w_in_all = _proj_weight(w_in)
    w_o_all = w_out.astype(BF16)
    w_up_all = ffn_w_up.astype(BF16)
    wd_all = ffn_w_down.astype(BF16)
    cw_all = jnp.swapaxes(ffn_conv_w, 1, 2)
    cb_all = ffn_conv_b.reshape(depth, 1, D_FF)
    fw = final_norm_w.reshape(1, D_MODEL)

    x_lat, x_ctx = x, ctx
    for l in range(depth):
        with_ctx = l < depth - 1
        m_lat = mods_all[l, :bsz].reshape(bsz, 6, D_MODEL)
        m_ctx = jnp.broadcast_to(mods_all[l, bsz].reshape(1, 6, D_MODEL), (bsz, 6, D_MODEL))
        mods = jnp.pad(jnp.stack([m_lat, m_ctx], axis=1), ((0, 0), (0, 0), (0, 2), (0, 0)))

        wq, wkv = _mla_weights(mla_w_uq[l], mla_w_ukv[l])
        qa, ka, va, pb, pdt, qc, kc, vc, qd, kd, vd = _proj_call(
            x_lat, x_ctx, norm1_w[l].reshape(1, D_MODEL), mods, l, w_in_all, tables,
            jnp.tile(attn_q_norm[l], 2).reshape(1, LANES), jnp.tile(attn_k_norm[l], 2).reshape(1, LANES),
            _pad_cols(mla_q_norm[l].reshape(1, MLA_Q_RANK), 2 * LANES), mla_kv_norm[l].reshape(1, MLA_KV_RANK),
            wq, wkv)

        ya = _dense_attn_call(qa, ka, va, None, (0, 0, 0, 0), with_ctx, "attn_gqa")
        ym = _dense_attn_call(qd, kd, vd, None, (0, 1, 2, 3), with_ctx, "attn_mla")
        yw = _win_attn_call(qc, kc, vc, win_sink[l], with_ctx)
        yb = _ssd_call(
            pb, pdt, ssm_conv_w[l].T, ssm_conv_b[l].reshape(1, SSM_CONV_CH),
            jnp.tile(jnp.broadcast_to(ssm_dt_bias[l].reshape(8, 1), (8, SSD_CHUNK)), (TT // SSD_CHUNK, 1)),
            jnp.tile(jnp.broadcast_to(ssm_a_log[l].reshape(8, 1), (8, SSD_CHUNK)), (TT // SSD_CHUNK, 1)),
            jnp.repeat(ssm_d[l], HEAD_DIM).reshape(1, GROUP_W), ssm_norm_w[l].reshape(1, GROUP_W))

        ys = (ya, yb, yw, ym)
        n2 = norm2_w[l].reshape(1, D_MODEL)
        last = l == depth - 1
        ffn_w = (w_o_all, n2, w_up_all, cw_all, cb_all, wd_all, fw)
        x_lat_next = _ffn_call(ys, x_lat, mods, l, *ffn_w, FFN_TILE, False, last)
        if with_ctx:
            x_ctx = _ffn_call(ys, x_ctx, mods, l, *ffn_w, CTX_LEN, True, False)
        x_lat = x_lat_next
    return x_lat
```

```python
import functools

import numpy as np
import jax
import jax.numpy as jnp
from jax import lax
from jax.experimental import pallas as pl
from jax.experimental.pallas import tpu as pltpu

F32 = jnp.float32
BF16 = jnp.bfloat16

D_MODEL = 1024
SEQ = 2048
CTX_LEN = 256
TT = SEQ + CTX_LEN
GRID_W = 64
ROPE_THETA = 10000.0
NORM_EPS = 1e-6
HEAD_DIM = 64
GROUP_W = 256
WINDOW = 128
SSM_STATE = 128
SSM_CONV_CH = 768
MLA_NOPE = 64
MLA_ROPE = 32
MLA_Q_RANK = 192
MLA_KV_RANK = 128
D_FF = 2816

LANES = 128
BF16_ROWS = 16
TOK_TILE = 256
N_LAT_TILES = SEQ // TOK_TILE
N_TILES = TT // TOK_TILE
SSD_CHUNK = 128
FFN_TILE = 1024
FFN_CHUNK = 256
FFN_HALO = 16
LOG2E = 1.4426950408889634

PA_W, PB_W, PDT_W, PC_W, PD_W = 512, 1024, 128, 512, 512
OFF_A = 0
OFF_B = OFF_A + PA_W
OFF_DT = OFF_B + PB_W
OFF_C = OFF_DT + PDT_W
OFF_D = OFF_C + PC_W
PROJ_W = OFF_D + PD_W


def _resident(shape, index=None):
    index = (0,) * len(shape) if index is None else tuple(index)
    return pl.BlockSpec(shape, lambda *_: index, pipeline_mode=pl.Buffered(1))


def _params(vmem_mb, ndims=2):
    return pltpu.CompilerParams(
        dimension_semantics=("arbitrary",) * ndims,
        vmem_limit_bytes=vmem_mb * 1024 * 1024,
    )


def _sigmoid(v):
    return 1.0 / (1.0 + jnp.exp(-v))


def _dot(a, b):
    return jnp.dot(a, b, preferred_element_type=F32)


def _dot_nt(a, b):
    return lax.dot_general(a, b, (((1,), (1,)), ((), ())), preferred_element_type=F32)


def _split3(v):
    t1 = v.astype(BF16)
    r1 = v - t1.astype(F32)
    t2 = r1.astype(BF16)
    t3 = (r1 - t2.astype(F32)).astype(BF16)
    return t1, t2, t3


def _mod_kernel(c_ref, w_ref, b_ref, o_ref):
    a = c_ref[...]
    s = a * _sigmoid(a)
    w = w_ref[...]
    s_hi = s.astype(BF16)
    s_lo = (s - s_hi.astype(F32)).astype(BF16)
    w_hi = w.astype(BF16)
    w_lo = (w - w_hi.astype(F32)).astype(BF16)
    o_ref[...] = _dot(s_hi, w_hi) + _dot(s_hi, w_lo) + _dot(s_lo, w_hi) + b_ref[...]


def _mod_call(cvec, w_mod, b_mod):
    n_layers, d, n6 = w_mod.shape
    rows = cvec.shape[0]
    tn = 1024
    return pl.pallas_call(
        _mod_kernel,
        grid=(n_layers, n6 // tn),
        in_specs=[
            pl.BlockSpec((rows, d), lambda l, j: (0, 0)),
            pl.BlockSpec((None, d, tn), lambda l, j: (l, 0, j)),
            pl.BlockSpec((None, 1, tn), lambda l, j: (l, 0, j)),
        ],
        out_specs=pl.BlockSpec((None, rows, tn), lambda l, j: (l, 0, j)),
        out_shape=jax.ShapeDtypeStruct((n_layers, rows, n6), F32),
        compiler_params=_params(40),
        name="mod",
    )(cvec, w_mod, b_mod)


def _rope(x, cos, sin_signed, half):
    lane = lax.broadcasted_iota(jnp.int32, x.shape, 1)
    first = (lane & (2 * half - 1)) < half
    partner = jnp.where(first, pltpu.roll(x, LANES - half, 1), pltpu.roll(x, half, 1))
    return x * cos + partner * sin_signed


def _axial_tables(rot_dim, lane_lo, reps):
    half = rot_dim // 2
    t = np.arange(SEQ)
    row = (t // GRID_W).astype(np.float32)
    col = (t % GRID_W).astype(np.float32)
    inv_freq = np.power(np.float32(ROPE_THETA), -np.arange(0, half, 2, dtype=np.float32) / np.float32(half))
    inv_freq = inv_freq.astype(np.float32)
    ang_r = (row[:, None] * inv_freq[None, :]).astype(np.float32)
    ang_c = (col[:, None] * inv_freq[None, :]).astype(np.float32)
    ang = np.concatenate([ang_r, ang_r, ang_c, ang_c], axis=-1).astype(np.float64)
    cos = np.cos(ang)
    sin = np.sin(ang)
    quarter = half // 2
    sign = np.where((np.arange(rot_dim) % half) < quarter, -1.0, 1.0)
    cos_t = np.ones((TT, LANES), np.float64)
    sin_t = np.zeros((TT, LANES), np.float64)
    for r in range(reps):
        lo = lane_lo + r * rot_dim
        cos_t[:SEQ, lo:lo + rot_dim] = cos
        sin_t[:SEQ, lo:lo + rot_dim] = sin * sign[None, :]
    return jnp.asarray(cos_t, F32), jnp.asarray(sin_t, F32)


def _store_value_slabs(pair, swapped, lo, v_ref):
    v_ref[0] = jnp.where(lo, pair, 1.0).astype(BF16)
    v_ref[1] = jnp.where(lo, 1.0, swapped).astype(BF16)
    v_ref[2] = jnp.where(lo, swapped, 1.0).astype(BF16)
    v_ref[3] = jnp.where(lo, 1.0, pair).astype(BF16)


def _gqa_prep(p, cos, sin, qn, kn, do_norm, q_ref, k_ref, v_ref):
    lane = lax.broadcasted_iota(jnp.int32, (p.shape[0], LANES), 1)
    lo = lane < HEAD_DIM

    def head_norm(v, gain):
        v2 = v * v
        s_lo = jnp.sum(jnp.where(lo, v2, 0.0), axis=-1, keepdims=True)
        s_hi = jnp.sum(jnp.where(lo, 0.0, v2), axis=-1, keepdims=True)
        ms = jnp.where(lo, s_lo, s_hi) * (1.0 / HEAD_DIM)
        return v * lax.rsqrt(ms + NORM_EPS) * gain

    scale = HEAD_DIM ** -0.5 * LOG2E
    for j in range(2):
        v = p[:, j * LANES:(j + 1) * LANES]
        if do_norm:
            v = head_norm(v, qn)
        v = _rope(v, cos, sin, HEAD_DIM // 4) * scale
        sw = pltpu.roll(v, HEAD_DIM, 1)
        if j == 0:
            q_ref[:, 0:LANES] = jnp.where(lo, v, 0.0).astype(BF16)
            q_ref[:, LANES:2 * LANES] = jnp.where(lo, sw, 0.0).astype(BF16)
        else:
            q_ref[:, 2 * LANES:3 * LANES] = jnp.where(lo, 0.0, sw).astype(BF16)
            q_ref[:, 3 * LANES:4 * LANES] = jnp.where(lo, 0.0, v).astype(BF16)

    k = p[:, 2 * LANES:3 * LANES]
    if do_norm:
        k = head_norm(k, kn)
    k_ref[0] = _rope(k, cos, sin, HEAD_DIM // 4).astype(BF16)

    vv = p[:, 3 * LANES:4 * LANES]
    _store_value_slabs(vv, pltpu.roll(vv, HEAD_DIM, 1), lo, v_ref)


def _mla_prep(p, cos, sin, gq, gkv, wq_ref, wkv_ref, q_ref, k_ref, v_ref):
    half = MLA_ROPE // 4
    cq = p[:, 0:2 * LANES]
    ms = jnp.sum(cq * cq, axis=-1, keepdims=True) * (1.0 / MLA_Q_RANK)
    cqn = (cq * lax.rsqrt(ms + NORM_EPS) * gq).astype(BF16)
    q = _dot(cqn, wq_ref[...])
    scale = (MLA_NOPE + MLA_ROPE) ** -0.5 * LOG2E
    for h in range(4):
        qh = _rope(q[:, h * LANES:(h + 1) * LANES], cos, sin, half) * scale
        q_ref[:, h * LANES:(h + 1) * LANES] = qh.astype(BF16)

    ckv = p[:, 2 * LANES:3 * LANES]
    ms = jnp.mean(ckv * ckv, axis=-1, keepdims=True)
    ckvn = (ckv * lax.rsqrt(ms + NORM_EPS) * gkv).astype(BF16)
    kv = _dot(ckvn, wkv_ref[...])
    kr = _rope(p[:, 3 * LANES:4 * LANES], cos, sin, half)
    for h in range(4):
        k_ref[h] = (kv[:, h * LANES:(h + 1) * LANES] + kr).astype(BF16)
    lane = lax.broadcasted_iota(jnp.int32, (p.shape[0], LANES), 1)
    lo = lane < HEAD_DIM
    v01 = kv[:, 4 * LANES:5 * LANES]
    v23 = kv[:, 5 * LANES:6 * LANES]
    v_ref[0] = jnp.where(lo, v01, 1.0).astype(BF16)
    v_ref[1] = jnp.where(lo, 1.0, v01).astype(BF16)
    v_ref[2] = jnp.where(lo, v23, 1.0).astype(BF16)
    v_ref[3] = jnp.where(lo, 1.0, v23).astype(BF16)


PROJ_SUB = 3
PROJ_STEPS = N_TILES // PROJ_SUB


def _proj_kernel(xl0_ref, xl1_ref, xl2_ref, xc_ref, nw_ref, m_ref, w_ref, cosa_ref, sina_ref, cosm_ref, sinm_ref,
                 qn_ref, kn_ref, gq_ref, gkv_ref, wq_ref, wkv_ref,
                 qa_ref, ka_ref, va_ref, pb_ref, pdt_ref, qc_ref, kc_ref, vc_ref, qd_ref, kd_ref, vd_ref):
    last_step = pl.program_id(1) == PROJ_STEPS - 1
    x_refs = (xl0_ref, xl1_ref, xl2_ref)
    for u in range(PROJ_SUB):
        rows = pl.ds(u * TOK_TILE, TOK_TILE)
        if u == PROJ_SUB - 1:
            x = jnp.where(last_step, xc_ref[...], x_refs[u][...])
            m = jnp.where(last_step, m_ref[1], m_ref[0])
        else:
            x = x_refs[u][...]
            m = m_ref[0]
        ms = jnp.mean(x * x, axis=-1, keepdims=True)
        y = x * lax.rsqrt(ms + NORM_EPS) * nw_ref[...]
        h = (y * (1.0 + m[1:2, :]) + m[0:1, :]).astype(BF16)
        cosa = cosa_ref[rows, :]
        sina = sina_ref[rows, :]
        _mla_prep(_dot(h, w_ref[:, OFF_D:OFF_D + PD_W]), cosm_ref[rows, :], sinm_ref[rows, :], gq_ref[...],
                  gkv_ref[...], wq_ref, wkv_ref, qd_ref.at[rows, :], kd_ref.at[:, rows, :], vd_ref.at[:, rows, :])
        _gqa_prep(_dot(h, w_ref[:, OFF_A:OFF_A + PA_W]), cosa, sina, qn_ref[...], kn_ref[...], True,
                  qa_ref.at[rows, :], ka_ref.at[:, rows, :], va_ref.at[:, rows, :])
        _gqa_prep(_dot(h, w_ref[:, OFF_C:OFF_C + PC_W]), cosa, sina, None, None, False,
                  qc_ref.at[rows, :], kc_ref.at[:, rows, :], vc_ref.at[:, rows, :])
        pdt_ref[rows, :] = _dot(h, w_ref[:, OFF_DT:OFF_DT + PDT_W])
        pb_ref[rows, :] = _dot(h, w_ref[:, OFF_B:OFF_B + PB_W]).astype(BF16)


def _proj_call(x_lat, x_ctx, norm_w, mods, layer, w, tables, qn, kn, gq, gkv, wq, wkv):
    b = x_lat.shape[0]
    step_rows = PROJ_SUB * TOK_TILE
    tok = lambda width: pl.BlockSpec((None, step_rows, width), lambda bi, i: (bi, i, 0))
    slab = lambda n: pl.BlockSpec((None, n, step_rows, LANES), lambda bi, i: (bi, 0, i, 0))
    table = pl.BlockSpec((step_rows, LANES), lambda bi, i: (i, 0))
    x_tile = lambda u: pl.BlockSpec(
        (None, TOK_TILE, D_MODEL), lambda bi, i: (bi, jnp.minimum(i * PROJ_SUB + u, N_LAT_TILES - 1), 0))
    q_shape = jax.ShapeDtypeStruct((b, TT, 4 * LANES), BF16)
    slab_shape = lambda n: jax.ShapeDtypeStruct((b, n, TT, LANES), BF16)
    return pl.pallas_call(
        _proj_kernel,
        grid=(b, PROJ_STEPS),
        in_specs=[
            x_tile(0), x_tile(1), x_tile(2),
            pl.BlockSpec((None, TOK_TILE, D_MODEL), lambda bi, i: (bi, 0, 0)),
            _resident((1, D_MODEL)),
            pl.BlockSpec((None, 2, 8, D_MODEL), lambda bi, i: (bi, 0, 0, 0)),
            _resident((None, D_MODEL, PROJ_W), (layer, 0, 0)),
            table, table, table, table,
            _resident((1, LANES)), _resident((1, LANES)),
            _resident((1, 2 * LANES)), _resident((1, LANES)),
            _resident((2 * LANES, 4 * LANES)), _resident((LANES, 6 * LANES)),
        ],
        out_specs=[tok(4 * LANES), slab(1), slab(4), tok(PB_W), tok(PDT_W),
                   tok(4 * LANES), slab(1), slab(4), tok(4 * LANES), slab(4), slab(4)],
        out_shape=[q_shape, slab_shape(1), slab_shape(4),
                   jax.ShapeDtypeStruct((b, TT, PB_W), BF16), jax.ShapeDtypeStruct((b, TT, PDT_W), F32),
                   q_shape, slab_shape(1), slab_shape(4), q_shape, slab_shape(4), slab_shape(4)],
        compiler_params=_params(48),
        name="proj",
    )(x_lat, x_lat, x_lat, x_ctx, norm_w, mods, w, *tables, qn, kn, gq, gkv, wq, wkv)


def _softmax_pv(s_parts, v_parts, sink):
    m = s_parts[0].max(axis=-1, keepdims=True)
    for s in s_parts[1:]:
        m = jnp.maximum(m, s.max(axis=-1, keepdims=True))
    if sink is not None:
        m = jnp.maximum(m, sink)
    acc = None
    for s, v in zip(s_parts, v_parts):
        o = _dot(jnp.exp2(s - m).astype(BF16), v)
        acc = o if acc is None else acc + o
    den = pltpu.roll(acc, HEAD_DIM, 1)
    if sink is not None:
        den = den + jnp.exp2(sink - m)
    return acc / den


def _merge_heads(outs, store):
    lane = lax.broadcasted_iota(jnp.int32, outs[0].shape, 1)
    lo = lane < HEAD_DIM
    store(0, jnp.where(lo, outs[0], outs[1]).astype(BF16))
    store(1, jnp.where(lo, outs[2], outs[3]).astype(BF16))


ATT_TILE = 256
DENSE_TILE = 512


def _dense_attn_kernel(*refs, k_of, has_sink, with_ctx):
    if has_sink:
        sink_ref, q_ref, k_ref, v_ref, o_ref = refs
    else:
        q_ref, k_ref, v_ref, o_ref = refs
        sink_ref = None

    def tile(r0, rows, k_lo, k_hi):
        def scores(h):
            return _dot_nt(q_ref[pl.ds(r0, rows), h * LANES:(h + 1) * LANES], k_ref[k_of[h], k_lo:k_hi, :])

        def store(j, val):
            o_ref[pl.ds(r0, rows), j * LANES:(j + 1) * LANES] = val

        outs = []
        s_next = scores(0)
        for h in range(4):
            s = s_next
            if h < 3:
                s_next = scores(h + 1)
            sink = sink_ref[h] * LOG2E if has_sink else None
            outs.append(_softmax_pv([s], [v_ref[h, k_lo:k_hi, :]], sink))
        _merge_heads(outs, store)

    def body(t, carry):
        for u in range(2):
            tile(pl.multiple_of((2 * t + u) * DENSE_TILE, DENSE_TILE), DENSE_TILE, 0, TT)
        return carry

    lax.fori_loop(0, SEQ // DENSE_TILE // 2, body, 0)
    if with_ctx:
        tile(SEQ, CTX_LEN, SEQ, TT)
    else:
        o_ref[SEQ:TT, :] = jnp.zeros((CTX_LEN, GROUP_W), BF16)


def _dense_attn_call(q, k, v, sink, k_of, with_ctx, name):
    b = q.shape[0]
    n_k = k.shape[1]
    has_sink = sink is not None
    in_specs = [
        pl.BlockSpec((None, TT, 4 * LANES), lambda bi: (bi, 0, 0)),
        pl.BlockSpec((None, n_k, TT, LANES), lambda bi: (bi, 0, 0, 0)),
        pl.BlockSpec((None, 4, TT, LANES), lambda bi: (bi, 0, 0, 0)),
    ]
    args = [q, k, v]
    if has_sink:
        in_specs = [pl.BlockSpec(memory_space=pltpu.SMEM)] + in_specs
        args = [sink] + args
    return pl.pallas_call(
        functools.partial(_dense_attn_kernel, k_of=k_of, has_sink=has_sink, with_ctx=with_ctx),
        grid=(b,),
        in_specs=in_specs,
        out_specs=pl.BlockSpec((None, TT, GROUP_W), lambda bi: (bi, 0, 0)),
        out_shape=jax.ShapeDtypeStruct((b, TT, GROUP_W), BF16),
        compiler_params=_params(48, 1),
        name=name,
    )(*args)


WIN_BAND = ATT_TILE + 2 * WINDOW


def _win_attn_kernel(sink_ref, q_ref, k_ref, v_ref, o_ref, *, with_ctx):
    k_ctx = k_ref[0, SEQ:TT, :]

    def finish(r0, rows, outs):
        def store(j, val):
            o_ref[pl.ds(r0, rows), j * LANES:(j + 1) * LANES] = val
        _merge_heads(outs, store)

    def band_tile(t):
        r0 = pl.multiple_of(t * ATT_TILE, ATT_TILE)
        start = pl.multiple_of(jnp.clip(r0 - WINDOW, 0, SEQ - WIN_BAND), WINDOW)
        qpos = r0 + lax.broadcasted_iota(jnp.int32, (ATT_TILE, WIN_BAND), 0)
        kpos = start + lax.broadcasted_iota(jnp.int32, (ATT_TILE, WIN_BAND), 1)
        mask = jnp.abs(kpos - qpos) <= WINDOW
        k_band = k_ref[0, pl.ds(start, WIN_BAND), :]

        def scores(h):
            qh = q_ref[pl.ds(r0, ATT_TILE), h * LANES:(h + 1) * LANES]
            return [jnp.where(mask, _dot_nt(qh, k_band), -jnp.inf), _dot_nt(qh, k_ctx)]

        outs = []
        s_next = scores(0)
        for h in range(4):
            s_parts = s_next
            if h < 3:
                s_next = scores(h + 1)
            outs.append(_softmax_pv(
                s_parts,
                [v_ref[h, pl.ds(start, WIN_BAND), :], v_ref[h, SEQ:TT, :]],
                sink_ref[h] * LOG2E))
        finish(r0, ATT_TILE, outs)

    def body(t, carry):
        band_tile(2 * t)
        band_tile(2 * t + 1)
        return carry

    lax.fori_loop(0, SEQ // ATT_TILE // 2, body, 0)
    if with_ctx:
        outs = []
        for h in range(4):
            qh = q_ref[SEQ:TT, h * LANES:(h + 1) * LANES]
            outs.append(_softmax_pv([_dot_nt(qh, k_ctx)], [v_ref[h, SEQ:TT, :]], sink_ref[h] * LOG2E))
        finish(SEQ, CTX_LEN, outs)
    else:
        o_ref[SEQ:TT, :] = jnp.zeros((CTX_LEN, GROUP_W), BF16)


def _win_attn_call(q, k, v, sink, with_ctx):
    b = q.shape[0]
    return pl.pallas_call(
        functools.partial(_win_attn_kernel, with_ctx=with_ctx),
        grid=(b,),
        in_specs=[
            pl.BlockSpec(memory_space=pltpu.SMEM),
            pl.BlockSpec((None, TT, 4 * LANES), lambda bi: (bi, 0, 0)),
            pl.BlockSpec((None, 1, TT, LANES), lambda bi: (bi, 0, 0, 0)),
            pl.BlockSpec((None, 4, TT, LANES), lambda bi: (bi, 0, 0, 0)),
        ],
        out_specs=pl.BlockSpec((None, TT, GROUP_W), lambda bi: (bi, 0, 0)),
        out_shape=jax.ShapeDtypeStruct((b, TT, GROUP_W), BF16),
        compiler_params=_params(32, 1),
        name="attn_win",
    )(sink, q, k, v)


def _ssd_kernel(pb_ref, dt_ref, cw_ref, cb_ref, dtb_ref, alog_ref, dsk_ref, nw_ref, y_ref,
                xbc_s, st_s, yd_s, csc_s, dt_s, cs_s, w_s, ecs_s, etot_s, h_s):
    lc = SSD_CHUNK
    n_chunks = TT // lc
    n_lat = SEQ // lc
    n_ctx = CTX_LEN // lc
    row_i = lax.broadcasted_iota(jnp.int32, (lc, lc), 0)
    col_i = lax.broadcasted_iota(jnp.int32, (lc, lc), 1)
    top = row_i < HEAD_DIM
    keep_f = row_i <= col_i
    keep_b = row_i >= col_i

    def pair_rows(tab, j0):
        return jnp.where(top, tab[j0:j0 + 1, :], tab[j0 + 1:j0 + 2, :])

    pre = jnp.concatenate([dt_ref[c * lc:(c + 1) * lc, :].T[0:8, :] for c in range(n_chunks)], axis=0) + dtb_ref[...]
    dt_all = jnp.maximum(pre, 0.0) + jnp.log1p(jnp.exp(-jnp.abs(pre)))
    da = dt_all * (-jnp.exp(alog_ref[...]))
    lane_t = lax.broadcasted_iota(jnp.int32, da.shape, 1)
    row_t = lax.broadcasted_iota(jnp.int32, da.shape, 0)
    cs = da
    k = 1
    while k < lc:
        cs = cs + jnp.where(lane_t >= k, pltpu.roll(cs, k, 1), 0.0)
        k *= 2
    tot = cs[:, lc - 1:lc]
    cs_all = jnp.where((row_t & 7) < 4, cs, tot - cs + da)
    dt_s[...] = dt_all
    cs_s[...] = cs_all
    w_s[...] = jnp.exp(tot - cs_all)
    ecs_s[...] = jnp.exp(cs_all)
    etot_s[...] = jnp.broadcast_to(jnp.exp(tot), da.shape)
    for c in range(n_chunks):
        csc_s[c] = jnp.concatenate([cs_all[c * 8:(c + 1) * 8, :]] * (lc // 8), axis=0).T

    def chunk_a(c):
        r0 = pl.multiple_of(c * lc, lc)
        t0 = pl.multiple_of(c * 8, 8)
        first = jnp.logical_or(c == 0, c == n_lat)
        last = jnp.logical_or(c == n_lat - 1, c == n_chunks - 1)
        u = pb_ref[pl.ds(r0, lc), GROUP_W:GROUP_W + SSM_CONV_CH].astype(F32)
        rp = pl.multiple_of(jnp.maximum(r0 - BF16_ROWS, 0), BF16_ROWS)
        rn = pl.multiple_of(jnp.minimum(r0 + lc, TT - BF16_ROWS), BF16_ROWS)
        prev_blk = pb_ref[pl.ds(rp, BF16_ROWS), GROUP_W:GROUP_W + SSM_CONV_CH].astype(F32)
        next_blk = pb_ref[pl.ds(rn, BF16_ROWS), GROUP_W:GROUP_W + SSM_CONV_CH].astype(F32)
        prev_row = jnp.where(first, 0.0, prev_blk[BF16_ROWS - 1:BF16_ROWS, :])
        next_row = jnp.where(last, 0.0, next_blk[0:1, :])
        rows = lax.broadcasted_iota(jnp.int32, (lc, SSM_CONV_CH), 0)
        up = jnp.where(rows == 0, prev_row, pltpu.roll(u, 1, 0))
        un = jnp.where(rows == lc - 1, next_row, pltpu.roll(u, lc - 1, 0))
        v = cw_ref[0:1, :] * up + cw_ref[1:2, :] * u + cw_ref[2:3, :] * un + cb_ref[...]
        act = v * _sigmoid(v)
        xbc_s[pl.ds(r0, lc), :] = act
        xs = act[:, 0:GROUP_W]
        bm = act[:, GROUP_W:2 * GROUP_W]
        cm = act[:, 2 * GROUP_W:3 * GROUP_W]
        dt_t = dt_s[pl.ds(t0, 8), :]
        cs_t = cs_s[pl.ds(t0, 8), :]
        w_t = w_s[pl.ds(t0, 8), :]
        cs_cols = csc_s[c]

        for g in range(2):
            bm_g = bm[:, g * LANES:(g + 1) * LANES].astype(BF16)
            cm_g = cm[:, g * LANES:(g + 1) * LANES].astype(BF16)
            cb_t = _dot_nt(bm_g, cm_g)
            xs_t = xs[:, g * LANES:(g + 1) * LANES].T
            yd = None
            for d in range(2):
                j0 = d * 4 + 2 * g
                keep = keep_f if d == 0 else keep_b
                xdt_t = xs_t * pair_rows(dt_t, j0)
                xdt_b = xdt_t.astype(BF16)
                parts = []
                for hh in range(2):
                    j = j0 + hh
                    seg_t = cs_t[j:j + 1, :] - cs_cols[:, j:j + 1]
                    dec_t = jnp.exp(jnp.where(keep, seg_t, -jnp.inf))
                    parts.append(_dot(xdt_b[hh * HEAD_DIM:(hh + 1) * HEAD_DIM, :], (cb_t * dec_t).astype(BF16)))
                y_dg = jnp.concatenate(parts, axis=0)
                yd = y_dg if yd is None else yd + y_dg
                xw_t = xdt_t * pair_rows(w_t, j0)
                st_s[c * 4 + d * 2 + g] = _dot(xw_t.astype(BF16), bm_g)
            yd_s[c * 2 + g] = yd

    def phase_a(t, carry):
        for u in range(3):
            chunk_a(3 * t + u)
        return carry

    lax.fori_loop(0, n_chunks // 3, phase_a, 0)

    h_s[...] = jnp.zeros_like(h_s)

    def phase_b(base_chunk, n_seg):
        def body(t, carry):
            for d in range(2):
                c = base_chunk + t if d == 0 else base_chunk + n_seg - 1 - t
                e = etot_s[pl.ds(pl.multiple_of(c * 8, 8), 8), :]
                for g in range(2):
                    slot = c * 4 + d * 2 + g
                    s_c = st_s[slot]
                    h_in = h_s[d * 2 + g]
                    st_s[slot] = h_in
                    h_s[d * 2 + g] = h_in * pair_rows(e, d * 4 + 2 * g) + s_c
            return carry
        lax.fori_loop(0, n_seg, body, 0)

    phase_b(n_lat, n_ctx)
    phase_b(0, n_lat)

    def chunk_c(c):
        r0 = pl.multiple_of(c * lc, lc)
        xs = xbc_s[pl.ds(r0, lc), 0:GROUP_W]
        cm = xbc_s[pl.ds(r0, lc), 2 * GROUP_W:3 * GROUP_W]
        ecs = ecs_s[pl.ds(pl.multiple_of(c * 8, 8), 8), :]
        halves = []
        for g in range(2):
            cm_g = cm[:, g * LANES:(g + 1) * LANES].astype(BF16)
            y_t = yd_s[c * 2 + g]
            for d in range(2):
                h_in = st_s[c * 4 + d * 2 + g].astype(BF16)
                y_t = y_t + _dot_nt(h_in, cm_g) * pair_rows(ecs, d * 4 + 2 * g)
            halves.append(y_t.T)
        y = jnp.concatenate(halves, axis=1) + xs * dsk_ref[...]
        z = pb_ref[pl.ds(r0, lc), 0:GROUP_W].astype(F32)
        gte = y * (z * _sigmoid(z))
        ms = jnp.mean(gte * gte, axis=-1, keepdims=True)
        y_ref[pl.ds(r0, lc), :] = (gte * lax.rsqrt(ms + NORM_EPS) * nw_ref[...]).astype(BF16)

    def phase_c(t, carry):
        for u in range(6):
            chunk_c(6 * t + u)
        return carry

    lax.fori_loop(0, n_chunks // 6, phase_c, 0)


def _ssd_call(pb, pdt, cw, cb, dtb, alog, dsk, nw):
    b = pb.shape[0]
    n_chunks = TT // SSD_CHUNK
    return pl.pallas_call(
        _ssd_kernel,
        grid=(b,),
        in_specs=[
            pl.BlockSpec((None, TT, PB_W), lambda bi: (bi, 0, 0)),
            pl.BlockSpec((None, TT, PDT_W), lambda bi: (bi, 0, 0)),
            _resident((3, SSM_CONV_CH)),
            _resident((1, SSM_CONV_CH)),
            _resident((n_chunks * 8, SSD_CHUNK)),
            _resident((n_chunks * 8, SSD_CHUNK)),
            _resident((1, GROUP_W)),
            _resident((1, GROUP_W)),
        ],
        out_specs=pl.BlockSpec((None, TT, GROUP_W), lambda bi: (bi, 0, 0)),
        out_shape=jax.ShapeDtypeStruct((b, TT, GROUP_W), BF16),
        scratch_shapes=[
            pltpu.VMEM((TT, SSM_CONV_CH), F32),
            pltpu.VMEM((n_chunks * 4, LANES, SSM_STATE), F32),
            pltpu.VMEM((n_chunks * 2, LANES, SSD_CHUNK), F32),
            pltpu.VMEM((n_chunks, SSD_CHUNK, LANES), F32),
            pltpu.VMEM((n_chunks * 8, SSD_CHUNK), F32),
            pltpu.VMEM((n_chunks * 8, SSD_CHUNK), F32),
            pltpu.VMEM((n_chunks * 8, SSD_CHUNK), F32),
            pltpu.VMEM((n_chunks * 8, SSD_CHUNK), F32),
            pltpu.VMEM((n_chunks * 8, SSD_CHUNK), F32),
            pltpu.VMEM((4, LANES, SSM_STATE), F32),
        ],
        compiler_params=_params(48, 1),
        name="ssd",
    )(pb, pdt, cw, cb, dtb, alog, dsk, nw)


def _ffn_kernel(*refs, tm, final_norm):
    y_refs = refs[0:12]
    x_ref, xp_ref, xn_ref, m_ref, wo_ref, nw_ref, wa_ref, wg_ref, cw_ref, cb_ref, wd_ref, fw_ref = refs[12:24]
    o_ref, u_s = refs[24:26]
    i = pl.program_id(1)
    n_i = pl.num_programs(1)
    hl = FFN_HALO
    acc = None
    for j in range(4):
        ycat = jnp.concatenate([r[...] for r in y_refs[3 * j:3 * j + 3]], axis=0)
        d = _dot(ycat, wo_ref[j * GROUP_W:(j + 1) * GROUP_W, :])
        acc = d if acc is None else acc + d
    x1 = jnp.concatenate([x_ref[...], xp_ref[...], xn_ref[...]], axis=0) + m_ref[2:3, :] * acc
    ms = jnp.mean(x1 * x1, axis=-1, keepdims=True)
    y = x1 * lax.rsqrt(ms + NORM_EPS) * nw_ref[...]
    h = (y * (1.0 + m_ref[4:5, :]) + m_ref[3:4, :]).astype(BF16)
    h_mid = h[0:tm, :]
    rid = lax.broadcasted_iota(jnp.int32, (tm, FFN_CHUNK), 0)
    has_prev = i > 0
    has_next = i < n_i - 1

    for j in range(D_FF // FFN_CHUNK):
        c0 = j * FFN_CHUNK
        a = _dot(h_mid, wa_ref[:, pl.ds(c0, FFN_CHUNK)])
        g = _dot(h, wg_ref[:, pl.ds(c0, FFN_CHUNK)])
        g_mid = g[0:tm, :]
        row_before = jnp.where(has_prev, g[tm + hl - 1:tm + hl, :], 0.0)
        row_after = jnp.where(has_next, g[tm + hl:tm + hl + 1, :], 0.0)
        g_prev = jnp.where(rid == 0, row_before, pltpu.roll(g_mid, 1, 0))
        g_next = jnp.where(rid == tm - 1, row_after, pltpu.roll(g_mid, tm - 1, 0))
        cw = cw_ref[:, pl.ds(c0, FFN_CHUNK)]
        gc = cw[0:1, :] * g_prev + cw[1:2, :] * g_mid + cw[2:3, :] * g_next + cb_ref[:, pl.ds(c0, FFN_CHUNK)]
        u_s[:, pl.ds(c0, FFN_CHUNK)] = (a * (gc * _sigmoid(gc))).astype(BF16)

    out = x1[0:tm, :] + m_ref[5:6, :] * _dot(u_s[...], wd_ref[...])
    if final_norm:
        ms2 = jnp.mean(out * out, axis=-1, keepdims=True)
        out = out * lax.rsqrt(ms2 + NORM_EPS) * fw_ref[...]
    o_ref[...] = out


def _ffn_call(ys, x, mods, layer, wo, norm_w, w_up, cw, cb, wd, fw, tm, is_ctx, final_norm):
    b, t, _ = x.shape
    n_tiles = t // tm
    hl = FFN_HALO
    hb = tm // hl
    row0 = SEQ if is_ctx else 0
    mi = 1 if is_ctx else 0

    def triple(width, rows, base):
        off_m = base // tm
        off_h = base // hl
        last = rows // hl - 1
        return [
            pl.BlockSpec((None, tm, width), lambda bi, i: (bi, i + off_m, 0)),
            pl.BlockSpec((None, hl, width), lambda bi, i: (bi, jnp.maximum(i * hb + off_h - 1, 0), 0)),
            pl.BlockSpec((None, hl, width), lambda bi, i: (bi, jnp.minimum((i + 1) * hb + off_h, last), 0)),
        ]

    in_specs = []
    args = []
    for yv in ys:
        in_specs += triple(GROUP_W, TT, row0)
        args += [yv, yv, yv]
    in_specs += triple(D_MODEL, t, 0)
    args += [x, x, x]
    in_specs += [
        pl.BlockSpec((None, None, 8, D_MODEL), lambda bi, i: (bi, mi, 0, 0)),
        _resident((None, 4 * GROUP_W, D_MODEL), (layer, 0, 0)),
        _resident((1, D_MODEL)),
        _resident((None, D_MODEL, D_FF), (layer, 0, 0)),
        _resident((None, D_MODEL, D_FF), (layer, 0, 1)),
        _resident((None, 3, D_FF), (layer, 0, 0)),
        _resident((None, 1, D_FF), (layer, 0, 0)),
        _resident((None, D_FF, D_MODEL), (layer, 0, 0)),
        _resident((1, D_MODEL)),
    ]
    args += [mods, wo, norm_w, w_up, w_up, cw, cb, wd, fw]
    return pl.pallas_call(
        functools.partial(_ffn_kernel, tm=tm, final_norm=final_norm),
        grid=(b, n_tiles),
        in_specs=in_specs,
        out_specs=pl.BlockSpec((None, tm, D_MODEL), lambda bi, i: (bi, i, 0)),
        out_shape=jax.ShapeDtypeStruct((b, t, D_MODEL), F32),
        scratch_shapes=[pltpu.VMEM((tm, D_FF), BF16)],
        compiler_params=_params(60),
        name="ffn_ctx" if is_ctx else "ffn_lat",
    )(*args)


def _pad_cols(w, width):
    return jnp.pad(w, ((0, 0), (0, width - w.shape[1])))


def _proj_weight(w_in):
    a_cols = 512
    b_cols = GROUP_W + SSM_CONV_CH + 8
    c_cols = 512
    w_in = w_in.astype(BF16)
    wa = w_in[..., :a_cols]
    wb = w_in[..., a_cols:a_cols + b_cols]
    wc = w_in[..., a_cols + b_cols:a_cols + b_cols + c_cols]
    wd = w_in[..., a_cols + b_cols + c_cols:]
    zeros = lambda n: jnp.zeros(w_in.shape[:-1] + (n,), w_in.dtype)
    w_zx = wb[..., :GROUP_W + SSM_CONV_CH]
    w_dt = jnp.concatenate([wb[..., GROUP_W + SSM_CONV_CH:], zeros(PDT_W - 8)], axis=-1)
    w_cq = jnp.concatenate([wd[..., :MLA_Q_RANK], zeros(2 * LANES - MLA_Q_RANK)], axis=-1)
    w_ckv = wd[..., MLA_Q_RANK:MLA_Q_RANK + MLA_KV_RANK]
    w_kr = jnp.concatenate([zeros(MLA_NOPE), wd[..., MLA_Q_RANK + MLA_KV_RANK:],
                            zeros(LANES - MLA_NOPE - MLA_ROPE)], axis=-1)
    return jnp.concatenate([wa, w_zx, w_dt, wc, w_cq, w_ckv, w_kr], axis=-1)


def _mla_weights(w_uq_l, w_ukv_l):
    dq = MLA_NOPE + MLA_ROPE
    wq = w_uq_l.reshape(MLA_Q_RANK, 4, dq)
    wq = jnp.pad(wq, ((0, 2 * LANES - MLA_Q_RANK), (0, 0), (0, LANES - dq))).reshape(2 * LANES, 4 * LANES)
    wkv = w_ukv_l.reshape(MLA_KV_RANK, 4, MLA_NOPE + HEAD_DIM)
    wk = jnp.pad(wkv[:, :, :MLA_NOPE], ((0, 0), (0, 0), (0, LANES - MLA_NOPE))).reshape(MLA_KV_RANK, 4 * LANES)
    wv = wkv[:, :, MLA_NOPE:].reshape(MLA_KV_RANK, 4 * HEAD_DIM)
    return wq.astype(BF16), jnp.concatenate([wk, wv], axis=1).astype(BF16)


def kernel(x, c, ctx, c_ctx, norm1_w, w_mod, b_mod, w_in, attn_q_norm, attn_k_norm, ssm_conv_w, ssm_conv_b,
           ssm_dt_bias, ssm_a_log, ssm_d, ssm_norm_w, win_sink, mla_q_norm, mla_w_uq, mla_kv_norm, mla_w_ukv,
           w_out, norm2_w, ffn_w_up, ffn_conv_w, ffn_conv_b, ffn_w_down, final_norm_w):
    bsz = x.shape[0]
    depth = w_in.shape[0]
    assert x.shape[1:] == (SEQ, D_MODEL) and ctx.shape[1:] == (CTX_LEN, D_MODEL)

    tables = _axial_tables(HEAD_DIM, 0, 2) + _axial_tables(MLA_ROPE, MLA_NOPE, 1)

    n_rows = ((bsz + 1 + 7) // 8) * 8
    cvec = jnp.concatenate([c, c_ctx[None, :], jnp.zeros((n_rows - bsz - 1, D_MODEL), F32)], axis=0)
    mods_all = _mod_call(cvec, w_mod, b_mod.reshape(depth, 1, 6 * D_MODEL))

    w_in_all = _proj_weight(w_in)
    w_o_all = w_out.astype(BF16)
    w_up_all = ffn_w_up.astype(BF16)
    wd_all = ffn_w_down.astype(BF16)
    cw_all = jnp.swapaxes(ffn_conv_w, 1, 2)
    cb_all = ffn_conv_b.reshape(depth, 1, D_FF)
    fw = final_norm_w.reshape(1, D_MODEL)

    x_lat, x_ctx = x, ctx
    for l in range(depth):
        with_ctx = l < depth - 1
        m_lat = mods_all[l, :bsz].reshape(bsz, 6, D_MODEL)
        m_ctx = jnp.broadcast_to(mods_all[l, bsz].reshape(1, 6, D_MODEL), (bsz, 6, D_MODEL))
        mods = jnp.pad(jnp.stack([m_lat, m_ctx], axis=1), ((0, 0), (0, 0), (0, 2), (0, 0)))

        wq, wkv = _mla_weights(mla_w_uq[l], mla_w_ukv[l])
        qa, ka, va, pb, pdt, qc, kc, vc, qd, kd, vd = _proj_call(
            x_lat, x_ctx, norm1_w[l].reshape(1, D_MODEL), mods, l, w_in_all, tables,
            jnp.tile(attn_q_norm[l], 2).reshape(1, LANES), jnp.tile(attn_k_norm[l], 2).reshape(1, LANES),
            _pad_cols(mla_q_norm[l].reshape(1, MLA_Q_RANK), 2 * LANES), mla_kv_norm[l].reshape(1, MLA_KV_RANK),
            wq, wkv)

        ya = _dense_attn_call(qa, ka, va, None, (0, 0, 0, 0), with_ctx, "attn_gqa")
        ym = _dense_attn_call(qd, kd, vd, None, (0, 1, 2, 3), with_ctx, "attn_mla")
        yw = _win_attn_call(qc, kc, vc, win_sink[l], with_ctx)
        yb = _ssd_call(
            pb, pdt, ssm_conv_w[l].T, ssm_conv_b[l].reshape(1, SSM_CONV_CH),
            jnp.tile(jnp.broadcast_to(ssm_dt_bias[l].reshape(8, 1), (8, SSD_CHUNK)), (TT // SSD_CHUNK, 1)),
            jnp.tile(jnp.broadcast_to(ssm_a_log[l].reshape(8, 1), (8, SSD_CHUNK)), (TT // SSD_CHUNK, 1)),
            jnp.repeat(ssm_d[l], HEAD_DIM).reshape(1, GROUP_W), ssm_norm_w[l].reshape(1, GROUP_W))

        ys = (ya, yb, yw, ym)
        n2 = norm2_w[l].reshape(1, D_MODEL)
        last = l == depth - 1
        ffn_w = (w_o_all, n2, w_up_all, cw_all, cb_all, wd_all, fw)
        x_lat_next = _ffn_call(ys, x_lat, mods, l, *ffn_w, FFN_TILE, False, last)
        if with_ctx:
            x_ctx = _ffn_call(ys, x_ctx, mods, l, *ffn_w, CTX_LEN, True, False)
        x_lat = x_lat_next
    return x_lat
```

```python
import functools

import numpy as np
import jax
import jax.numpy as jnp
from jax import lax
from jax.experimental import pallas as pl
from jax.experimental.pallas import tpu as pltpu

F32 = jnp.float32
BF16 = jnp.bfloat16

D_MODEL = 1024
SEQ = 2048
CTX_LEN = 256
TT = SEQ + CTX_LEN
GRID_W = 64
ROPE_THETA = 10000.0
NORM_EPS = 1e-6
HEAD_DIM = 64
GROUP_W = 256
WINDOW = 128
SSM_STATE = 128
SSM_CONV_CH = 768
MLA_NOPE = 64
MLA_ROPE = 32
MLA_Q_RANK = 192
MLA_KV_RANK = 128
D_FF = 2816

LANES = 128
BF16_ROWS = 16
TOK_TILE = 256
N_LAT_TILES = SEQ // TOK_TILE
N_TILES = TT // TOK_TILE
SSD_CHUNK = 128
FFN_TILE = 1024
FFN_CHUNK = 256
FFN_HALO = 16
LOG2E = 1.4426950408889634

PA_W, PB_W, PDT_W, PC_W, PD_W = 512, 1024, 128, 512, 512
OFF_A = 0
OFF_B = OFF_A + PA_W
OFF_DT = OFF_B + PB_W
OFF_C = OFF_DT + PDT_W
OFF_D = OFF_C + PC_W
PROJ_W = OFF_D + PD_W


def _resident(shape, index=None):
    index = (0,) * len(shape) if index is None else tuple(index)
    return pl.BlockSpec(shape, lambda *_: index, pipeline_mode=pl.Buffered(1))


def _params(vmem_mb, ndims=2):
    return pltpu.CompilerParams(
        dimension_semantics=("arbitrary",) * ndims,
        vmem_limit_bytes=vmem_mb * 1024 * 1024,
    )


def _sigmoid(v):
    return 1.0 / (1.0 + jnp.exp(-v))


def _dot(a, b):
    return jnp.dot(a, b, preferred_element_type=F32)


def _dot_nt(a, b):
    return lax.dot_general(a, b, (((1,), (1,)), ((), ())), preferred_element_type=F32)


def _mod_kernel(c_ref, w_ref, b_ref, o_ref):
    a = c_ref[...]
    s = a * _sigmoid(a)
    w = w_ref[...]
    s_hi = s.astype(BF16)
    s_lo = (s - s_hi.astype(F32)).astype(BF16)
    w_hi = w.astype(BF16)
    w_lo = (w - w_hi.astype(F32)).astype(BF16)
    o_ref[...] = _dot(s_hi, w_hi) + _dot(s_hi, w_lo) + _dot(s_lo, w_hi) + b_ref[...]


def _mod_call(cvec, w_mod, b_mod):
    n_layers, d, n6 = w_mod.shape
    rows = cvec.shape[0]
    tn = 1024
    return pl.pallas_call(
        _mod_kernel,
        grid=(n_layers, n6 // tn),
        in_specs=[
            pl.BlockSpec((rows, d), lambda l, j: (0, 0)),
            pl.BlockSpec((None, d, tn), lambda l, j: (l, 0, j)),
            pl.BlockSpec((None, 1, tn), lambda l, j: (l, 0, j)),
        ],
        out_specs=pl.BlockSpec((None, rows, tn), lambda l, j: (l, 0, j)),
        out_shape=jax.ShapeDtypeStruct((n_layers, rows, n6), F32),
        compiler_params=_params(40),
        name="mod",
    )(cvec, w_mod, b_mod)


def _rope(x, cos, sin_signed, half):
    lane = lax.broadcasted_iota(jnp.int32, x.shape, 1)
    first = (lane & (2 * half - 1)) < half
    partner = jnp.where(first, pltpu.roll(x, LANES - half, 1), pltpu.roll(x, half, 1))
    return x * cos + partner * sin_signed


def _axial_tables(rot_dim, lane_lo, reps):
    half = rot_dim // 2
    t = np.arange(SEQ)
    row = (t // GRID_W).astype(np.float32)
    col = (t % GRID_W).astype(np.float32)
    inv_freq = np.power(np.float32(ROPE_THETA), -np.arange(0, half, 2, dtype=np.float32) / np.float32(half))
    inv_freq = inv_freq.astype(np.float32)
    ang_r = (row[:, None] * inv_freq[None, :]).astype(np.float32)
    ang_c = (col[:, None] * inv_freq[None, :]).astype(np.float32)
    ang = np.concatenate([ang_r, ang_r, ang_c, ang_c], axis=-1).astype(np.float64)
    cos = np.cos(ang)
    sin = np.sin(ang)
    quarter = half // 2
    sign = np.where((np.arange(rot_dim) % half) < quarter, -1.0, 1.0)
    cos_t = np.ones((TT, LANES), np.float64)
    sin_t = np.zeros((TT, LANES), np.float64)
    for r in range(reps):
        lo = lane_lo + r * rot_dim
        cos_t[:SEQ, lo:lo + rot_dim] = cos
        sin_t[:SEQ, lo:lo + rot_dim] = sin * sign[None, :]
    return jnp.asarray(cos_t, F32), jnp.asarray(sin_t, F32)


def _store_value_slabs(pair, swapped, lo, v_ref):
    v_ref[0] = jnp.where(lo, pair, 1.0).astype(BF16)
    v_ref[1] = jnp.where(lo, 1.0, swapped).astype(BF16)
    v_ref[2] = jnp.where(lo, swapped, 1.0).astype(BF16)
    v_ref[3] = jnp.where(lo, 1.0, pair).astype(BF16)


def _gqa_prep(p, cos, sin, qn, kn, do_norm, q_ref, k_ref, v_ref):
    lane = lax.broadcasted_iota(jnp.int32, (p.shape[0], LANES), 1)
    lo = lane < HEAD_DIM

    def head_norm(v, gain):
        v2 = v * v
        s_lo = jnp.sum(jnp.where(lo, v2, 0.0), axis=-1, keepdims=True)
        s_hi = jnp.sum(jnp.where(lo, 0.0, v2), axis=-1, keepdims=True)
        ms = jnp.where(lo, s_lo, s_hi) * (1.0 / HEAD_DIM)
        return v * lax.rsqrt(ms + NORM_EPS) * gain

    scale = HEAD_DIM ** -0.5 * LOG2E
    for j in range(2):
        v = p[:, j * LANES:(j + 1) * LANES]
        if do_norm:
            v = head_norm(v, qn)
        v = _rope(v, cos, sin, HEAD_DIM // 4) * scale
        sw = pltpu.roll(v, HEAD_DIM, 1)
        if j == 0:
            q_ref[:, 0:LANES] = jnp.where(lo, v, 0.0).astype(BF16)
            q_ref[:, LANES:2 * LANES] = jnp.where(lo, sw, 0.0).astype(BF16)
        else:
            q_ref[:, 2 * LANES:3 * LANES] = jnp.where(lo, 0.0, sw).astype(BF16)
            q_ref[:, 3 * LANES:4 * LANES] = jnp.where(lo, 0.0, v).astype(BF16)

    k = p[:, 2 * LANES:3 * LANES]
    if do_norm:
        k = head_norm(k, kn)
    k_ref[0] = _rope(k, cos, sin, HEAD_DIM // 4).astype(BF16)

    vv = p[:, 3 * LANES:4 * LANES]
    _store_value_slabs(vv, pltpu.roll(vv, HEAD_DIM, 1), lo, v_ref)


def _mla_prep(p, cos, sin, gq, gkv, wq_ref, wkv_ref, q_ref, k_ref, v_ref):
    half = MLA_ROPE // 4
    cq = p[:, 0:2 * LANES]
    ms = jnp.sum(cq * cq, axis=-1, keepdims=True) * (1.0 / MLA_Q_RANK)
    cqn = (cq * lax.rsqrt(ms + NORM_EPS) * gq).astype(BF16)
    q = _dot(cqn, wq_ref[...])
    scale = (MLA_NOPE + MLA_ROPE) ** -0.5 * LOG2E
    for h in range(4):
        qh = _rope(q[:, h * LANES:(h + 1) * LANES], cos, sin, half) * scale
        q_ref[:, h * LANES:(h + 1) * LANES] = qh.astype(BF16)

    ckv = p[:, 2 * LANES:3 * LANES]
    ms = jnp.mean(ckv * ckv, axis=-1, keepdims=True)
    ckvn = (ckv * lax.rsqrt(ms + NORM_EPS) * gkv).astype(BF16)
    kv = _dot(ckvn, wkv_ref[...])
    kr = _rope(p[:, 3 * LANES:4 * LANES], cos, sin, half)
    for h in range(4):
        k_ref[h] = (kv[:, h * LANES:(h + 1) * LANES] + kr).astype(BF16)
    lane = lax.broadcasted_iota(jnp.int32, (p.shape[0], LANES), 1)
    lo = lane < HEAD_DIM
    v01 = kv[:, 4 * LANES:5 * LANES]
    v23 = kv[:, 5 * LANES:6 * LANES]
    v_ref[0] = jnp.where(lo, v01, 1.0).astype(BF16)
    v_ref[1] = jnp.where(lo, 1.0, v01).astype(BF16)
    v_ref[2] = jnp.where(lo, v23, 1.0).astype(BF16)
    v_ref[3] = jnp.where(lo, 1.0, v23).astype(BF16)


PROJ_SUB = 3
PROJ_STEPS = N_TILES // PROJ_SUB


def _proj_kernel(xl0_ref, xl1_ref, xl2_ref, xc_ref, nw_ref, m_ref, w_ref, cosa_ref, sina_ref, cosm_ref, sinm_ref,
                 qn_ref, kn_ref, gq_ref, gkv_ref, wq_ref, wkv_ref,
                 qa_ref, ka_ref, va_ref, pb_ref, pdt_ref, qc_ref, kc_ref, vc_ref, qd_ref, kd_ref, vd_ref):
    last_step = pl.program_id(1) == PROJ_STEPS - 1
    x_refs = (xl0_ref, xl1_ref, xl2_ref)
    for u in range(PROJ_SUB):
        rows = pl.ds(u * TOK_TILE, TOK_TILE)
        if u == PROJ_SUB - 1:
            x = jnp.where(last_step, xc_ref[...], x_refs[u][...])
            m = jnp.where(last_step, m_ref[1], m_ref[0])
        else:
            x = x_refs[u][...]
            m = m_ref[0]
        ms = jnp.mean(x * x, axis=-1, keepdims=True)
        y = x * lax.rsqrt(ms + NORM_EPS) * nw_ref[...]
        h = (y * (1.0 + m[1:2, :]) + m[0:1, :]).astype(BF16)
        cosa = cosa_ref[rows, :]
        sina = sina_ref[rows, :]
        _mla_prep(_dot(h, w_ref[:, OFF_D:OFF_D + PD_W]), cosm_ref[rows, :], sinm_ref[rows, :], gq_ref[...],
                  gkv_ref[...], wq_ref, wkv_ref, qd_ref.at[rows, :], kd_ref.at[:, rows, :], vd_ref.at[:, rows, :])
        _gqa_prep(_dot(h, w_ref[:, OFF_A:OFF_A + PA_W]), cosa, sina, qn_ref[...], kn_ref[...], True,
                  qa_ref.at[rows, :], ka_ref.at[:, rows, :], va_ref.at[:, rows, :])
        _gqa_prep(_dot(h, w_ref[:, OFF_C:OFF_C + PC_W]), cosa, sina, None, None, False,
                  qc_ref.at[rows, :], kc_ref.at[:, rows, :], vc_ref.at[:, rows, :])
        pdt_ref[rows, :] = _dot(h, w_ref[:, OFF_DT:OFF_DT + PDT_W])
        pb_ref[rows, :] = _dot(h, w_ref[:, OFF_B:OFF_B + PB_W]).astype(BF16)


def _proj_call(x_lat, x_ctx, norm_w, mods, layer, w, tables, qn, kn, gq, gkv, wq, wkv):
    b = x_lat.shape[0]
    step_rows = PROJ_SUB * TOK_TILE
    tok = lambda width: pl.BlockSpec((None, step_rows, width), lambda bi, i: (bi, i, 0))
    slab = lambda n: pl.BlockSpec((None, n, step_rows, LANES), lambda bi, i: (bi, 0, i, 0))
    table = pl.BlockSpec((step_rows, LANES), lambda bi, i: (i, 0))
    x_tile = lambda u: pl.BlockSpec(
        (None, TOK_TILE, D_MODEL), lambda bi, i: (bi, jnp.minimum(i * PROJ_SUB + u, N_LAT_TILES - 1), 0))
    q_shape = jax.ShapeDtypeStruct((b, TT, 4 * LANES), BF16)
    slab_shape = lambda n: jax.ShapeDtypeStruct((b, n, TT, LANES), BF16)
    return pl.pallas_call(
        _proj_kernel,
        grid=(b, PROJ_STEPS),
        in_specs=[
            x_tile(0), x_tile(1), x_tile(2),
            pl.BlockSpec((None, TOK_TILE, D_MODEL), lambda bi, i: (bi, 0, 0)),
            _resident((1, D_MODEL)),
            pl.BlockSpec((None, 2, 8, D_MODEL), lambda bi, i: (bi, 0, 0, 0)),
            _resident((None, D_MODEL, PROJ_W), (layer, 0, 0)),
            table, table, table, table,
            _resident((1, LANES)), _resident((1, LANES)),
            _resident((1, 2 * LANES)), _resident((1, LANES)),
            _resident((2 * LANES, 4 * LANES)), _resident((LANES, 6 * LANES)),
        ],
        out_specs=[tok(4 * LANES), slab(1), slab(4), tok(PB_W), tok(PDT_W),
                   tok(4 * LANES), slab(1), slab(4), tok(4 * LANES), slab(4), slab(4)],
        out_shape=[q_shape, slab_shape(1), slab_shape(4),
                   jax.ShapeDtypeStruct((b, TT, PB_W), BF16), jax.ShapeDtypeStruct((b, TT, PDT_W), F32),
                   q_shape, slab_shape(1), slab_shape(4), q_shape, slab_shape(4), slab_shape(4)],
        compiler_params=_params(48),
        name="proj",
    )(x_lat, x_lat, x_lat, x_ctx, norm_w, mods, w, *tables, qn, kn, gq, gkv, wq, wkv)


def _softmax_pv(s_parts, v_parts, sink):
    m = s_parts[0].max(axis=-1, keepdims=True)
    for s in s_parts[1:]:
        m = jnp.maximum(m, s.max(axis=-1, keepdims=True))
    if sink is not None:
        m = jnp.maximum(m, sink)
    acc = None
    for s, v in zip(s_parts, v_parts):
        o = _dot(jnp.exp2(s - m).astype(BF16), v)
        acc = o if acc is None else acc + o
    den = pltpu.roll(acc, HEAD_DIM, 1)
    if sink is not None:
        den = den + jnp.exp2(sink - m)
    return acc / den


def _merge_heads(outs, store):
    lane = lax.broadcasted_iota(jnp.int32, outs[0].shape, 1)
    lo = lane < HEAD_DIM
    store(0, jnp.where(lo, outs[0], outs[1]).astype(BF16))
    store(1, jnp.where(lo, outs[2], outs[3]).astype(BF16))


ATT_TILE = 256
DENSE_TILE = 512


def _dense_attn_kernel(*refs, k_of, has_sink, with_ctx):
    if has_sink:
        sink_ref, q_ref, k_ref, v_ref, o_ref = refs
    else:
        q_ref, k_ref, v_ref, o_ref = refs
        sink_ref = None

    def tile(r0, rows, k_lo, k_hi):
        def scores(h):
            return _dot_nt(q_ref[pl.ds(r0, rows), h * LANES:(h + 1) * LANES], k_ref[k_of[h], k_lo:k_hi, :])

        def store(j, val):
            o_ref[pl.ds(r0, rows), j * LANES:(j + 1) * LANES] = val

        outs = []
        s_next = scores(0)
        for h in range(4):
            s = s_next
            if h < 3:
                s_next = scores(h + 1)
            sink = sink_ref[h] * LOG2E if has_sink else None
            outs.append(_softmax_pv([s], [v_ref[h, k_lo:k_hi, :]], sink))
        _merge_heads(outs, store)

    def body(t, carry):
        for u in range(2):
            tile(pl.multiple_of((2 * t + u) * DENSE_TILE, DENSE_TILE), DENSE_TILE, 0, TT)
        return carry

    lax.fori_loop(0, SEQ // DENSE_TILE // 2, body, 0)
    if with_ctx:
        tile(SEQ, CTX_LEN, SEQ, TT)
    else:
        o_ref[SEQ:TT, :] = jnp.zeros((CTX_LEN, GROUP_W), BF16)


def _dense_attn_call(q, k, v, sink, k_of, with_ctx, name):
    b = q.shape[0]
    n_k = k.shape[1]
    has_sink = sink is not None
    in_specs = [
        pl.BlockSpec((None, TT, 4 * LANES), lambda bi: (bi, 0, 0)),
        pl.BlockSpec((None, n_k, TT, LANES), lambda bi: (bi, 0, 0, 0)),
        pl.BlockSpec((None, 4, TT, LANES), lambda bi: (bi, 0, 0, 0)),
    ]
    args = [q, k, v]
    if has_sink:
        in_specs = [pl.BlockSpec(memory_space=pltpu.SMEM)] + in_specs
        args = [sink] + args
    return pl.pallas_call(
        functools.partial(_dense_attn_kernel, k_of=k_of, has_sink=has_sink, with_ctx=with_ctx),
        grid=(b,),
        in_specs=in_specs,
        out_specs=pl.BlockSpec((None, TT, GROUP_W), lambda bi: (bi, 0, 0)),
        out_shape=jax.ShapeDtypeStruct((b, TT, GROUP_W), BF16),
        compiler_params=_params(48, 1),
        name=name,
    )(*args)


WIN_BAND = ATT_TILE + 2 * WINDOW


def _win_attn_kernel(sink_ref, q_ref, k_ref, v_ref, o_ref, *, with_ctx):
    k_ctx = k_ref[0, SEQ:TT, :]

    def finish(r0, rows, outs):
        def store(j, val):
            o_ref[pl.ds(r0, rows), j * LANES:(j + 1) * LANES] = val
        _merge_heads(outs, store)

    def band_tile(t):
        r0 = pl.multiple_of(t * ATT_TILE, ATT_TILE)
        start = pl.multiple_of(jnp.clip(r0 - WINDOW, 0, SEQ - WIN_BAND), WINDOW)
        qpos = r0 + lax.broadcasted_iota(jnp.int32, (ATT_TILE, WIN_BAND), 0)
        kpos = start + lax.broadcasted_iota(jnp.int32, (ATT_TILE, WIN_BAND), 1)
        mask = jnp.abs(kpos - qpos) <= WINDOW
        k_band = k_ref[0, pl.ds(start, WIN_BAND), :]

        def scores(h):
            qh = q_ref[pl.ds(r0, ATT_TILE), h * LANES:(h + 1) * LANES]
            return [jnp.where(mask, _dot_nt(qh, k_band), -jnp.inf), _dot_nt(qh, k_ctx)]

        outs = []
        s_next = scores(0)
        for h in range(4):
            s_parts = s_next
            if h < 3:
                s_next = scores(h + 1)
            outs.append(_softmax_pv(
                s_parts,
                [v_ref[h, pl.ds(start, WIN_BAND), :], v_ref[h, SEQ:TT, :]],
                sink_ref[h] * LOG2E))
        finish(r0, ATT_TILE, outs)

    def body(t, carry):
        for u in range(4):
            band_tile(4 * t + u)
        return carry

    lax.fori_loop(0, SEQ // ATT_TILE // 4, body, 0)
    if with_ctx:
        outs = []
        for h in range(4):
            qh = q_ref[SEQ:TT, h * LANES:(h + 1) * LANES]
            outs.append(_softmax_pv([_dot_nt(qh, k_ctx)], [v_ref[h, SEQ:TT, :]], sink_ref[h] * LOG2E))
        finish(SEQ, CTX_LEN, outs)
    else:
        o_ref[SEQ:TT, :] = jnp.zeros((CTX_LEN, GROUP_W), BF16)


def _win_attn_call(q, k, v, sink, with_ctx):
    b = q.shape[0]
    return pl.pallas_call(
        functools.partial(_win_attn_kernel, with_ctx=with_ctx),
        grid=(b,),
        in_specs=[
            pl.BlockSpec(memory_space=pltpu.SMEM),
            pl.BlockSpec((None, TT, 4 * LANES), lambda bi: (bi, 0, 0)),
            pl.BlockSpec((None, 1, TT, LANES), lambda bi: (bi, 0, 0, 0)),
            pl.BlockSpec((None, 4, TT, LANES), lambda bi: (bi, 0, 0, 0)),
        ],
        out_specs=pl.BlockSpec((None, TT, GROUP_W), lambda bi: (bi, 0, 0)),
        out_shape=jax.ShapeDtypeStruct((b, TT, GROUP_W), BF16),
        compiler_params=_params(32, 1),
        name="attn_win",
    )(sink, q, k, v)


def _ssd_kernel(pb_ref, dt_ref, cw_ref, cb_ref, dtb_ref, alog_ref, dsk_ref, nw_ref, y_ref,
                xbc_s, st_s, yd_s, csc_s, dt_s, cs_s, w_s, ecs_s, etot_s, h_s):
    lc = SSD_CHUNK
    n_chunks = TT // lc
    n_lat = SEQ // lc
    n_ctx = CTX_LEN // lc
    row_i = lax.broadcasted_iota(jnp.int32, (lc, lc), 0)
    col_i = lax.broadcasted_iota(jnp.int32, (lc, lc), 1)
    top = row_i < HEAD_DIM
    keep_f = row_i <= col_i
    keep_b = row_i >= col_i

    def pair_rows(tab, j0):
        return jnp.where(top, tab[j0:j0 + 1, :], tab[j0 + 1:j0 + 2, :])

    pre = jnp.concatenate([dt_ref[c * lc:(c + 1) * lc, :].T[0:8, :] for c in range(n_chunks)], axis=0) + dtb_ref[...]
    dt_all = jnp.maximum(pre, 0.0) + jnp.log1p(jnp.exp(-jnp.abs(pre)))
    da = dt_all * (-jnp.exp(alog_ref[...]))
    lane_t = lax.broadcasted_iota(jnp.int32, da.shape, 1)
    row_t = lax.broadcasted_iota(jnp.int32, da.shape, 0)
    cs = da
    k = 1
    while k < lc:
        cs = cs + jnp.where(lane_t >= k, pltpu.roll(cs, k, 1), 0.0)
        k *= 2
    tot = cs[:, lc - 1:lc]
    cs_all = jnp.where((row_t & 7) < 4, cs, tot - cs + da)
    dt_s[...] = dt_all
    cs_s[...] = cs_all
    w_s[...] = jnp.exp(tot - cs_all)
    ecs_s[...] = jnp.exp(cs_all)
    etot_s[...] = jnp.broadcast_to(jnp.exp(tot), da.shape)
    for c in range(n_chunks):
        csc_s[c] = jnp.concatenate([cs_all[c * 8:(c + 1) * 8, :]] * (lc // 8), axis=0).T

    def chunk_a(c):
        r0 = pl.multiple_of(c * lc, lc)
        t0 = pl.multiple_of(c * 8, 8)
        first = jnp.logical_or(c == 0, c == n_lat)
        last = jnp.logical_or(c == n_lat - 1, c == n_chunks - 1)
        u = pb_ref[pl.ds(r0, lc), GROUP_W:GROUP_W + SSM_CONV_CH].astype(F32)
        rp = pl.multiple_of(jnp.maximum(r0 - BF16_ROWS, 0), BF16_ROWS)
        rn = pl.multiple_of(jnp.minimum(r0 + lc, TT - BF16_ROWS), BF16_ROWS)
        prev_blk = pb_ref[pl.ds(rp, BF16_ROWS), GROUP_W:GROUP_W + SSM_CONV_CH].astype(F32)
        next_blk = pb_ref[pl.ds(rn, BF16_ROWS), GROUP_W:GROUP_W + SSM_CONV_CH].astype(F32)
        prev_row = jnp.where(first, 0.0, prev_blk[BF16_ROWS - 1:BF16_ROWS, :])
        next_row = jnp.where(last, 0.0, next_blk[0:1, :])
        rows = lax.broadcasted_iota(jnp.int32, (lc, SSM_CONV_CH), 0)
        up = jnp.where(rows == 0, prev_row, pltpu.roll(u, 1, 0))
        un = jnp.where(rows == lc - 1, next_row, pltpu.roll(u, lc - 1, 0))
        v = cw_ref[0:1, :] * up + cw_ref[1:2, :] * u + cw_ref[2:3, :] * un + cb_ref[...]
        act = v * _sigmoid(v)
        xbc_s[pl.ds(r0, lc), :] = act
        xs = act[:, 0:GROUP_W]
        bm = act[:, GROUP_W:2 * GROUP_W]
        cm = act[:, 2 * GROUP_W:3 * GROUP_W]
        dt_t = dt_s[pl.ds(t0, 8), :]
        cs_t = cs_s[pl.ds(t0, 8), :]
        w_t = w_s[pl.ds(t0, 8), :]
        cs_cols = csc_s[c]

        for g in range(2):
            bm_g = bm[:, g * LANES:(g + 1) * LANES].astype(BF16)
            cm_g = cm[:, g * LANES:(g + 1) * LANES].astype(BF16)
            cb_t = _dot_nt(bm_g, cm_g)
            xs_t = xs[:, g * LANES:(g + 1) * LANES].T
            yd = None
            for d in range(2):
                j0 = d * 4 + 2 * g
                keep = keep_f if d == 0 else keep_b
                xdt_t = xs_t * pair_rows(dt_t, j0)
                xdt_b = xdt_t.astype(BF16)
                parts = []
                for hh in range(2):
                    j = j0 + hh
                    seg_t = cs_t[j:j + 1, :] - cs_cols[:, j:j + 1]
                    dec_t = jnp.exp(jnp.where(keep, seg_t, -jnp.inf))
                    parts.append(_dot(xdt_b[hh * HEAD_DIM:(hh + 1) * HEAD_DIM, :], (cb_t * dec_t).astype(BF16)))
                y_dg = jnp.concatenate(parts, axis=0)
                yd = y_dg if yd is None else yd + y_dg
                xw_t = xdt_t * pair_rows(w_t, j0)
                st_s[c * 4 + d * 2 + g] = _dot(xw_t.astype(BF16), bm_g)
            yd_s[c * 2 + g] = yd

    def phase_a(t, carry):
        for u in range(3):
            chunk_a(3 * t + u)
        return carry

    lax.fori_loop(0, n_chunks // 3, phase_a, 0)

    h_s[...] = jnp.zeros_like(h_s)

    def phase_b(base_chunk, n_seg):
        def body(t, carry):
            for d in range(2):
                c = base_chunk + t if d == 0 else base_chunk + n_seg - 1 - t
                e = etot_s[pl.ds(pl.multiple_of(c * 8, 8), 8), :]
                for g in range(2):
                    slot = c * 4 + d * 2 + g
                    s_c = st_s[slot]
                    h_in = h_s[d * 2 + g]
                    st_s[slot] = h_in
                    h_s[d * 2 + g] = h_in * pair_rows(e, d * 4 + 2 * g) + s_c
            return carry
        lax.fori_loop(0, n_seg, body, 0)

    phase_b(n_lat, n_ctx)
    phase_b(0, n_lat)

    def chunk_c(c):
        r0 = pl.multiple_of(c * lc, lc)
        xs = xbc_s[pl.ds(r0, lc), 0:GROUP_W]
        cm = xbc_s[pl.ds(r0, lc), 2 * GROUP_W:3 * GROUP_W]
        ecs = ecs_s[pl.ds(pl.multiple_of(c * 8, 8), 8), :]
        halves = []
        for g in range(2):
            cm_g = cm[:, g * LANES:(g + 1) * LANES].astype(BF16)
            y_t = yd_s[c * 2 + g]
            for d in range(2):
                h_in = st_s[c * 4 + d * 2 + g].astype(BF16)
                y_t = y_t + _dot_nt(h_in, cm_g) * pair_rows(ecs, d * 4 + 2 * g)
            halves.append(y_t.T)
        y = jnp.concatenate(halves, axis=1) + xs * dsk_ref[...]
        z = pb_ref[pl.ds(r0, lc), 0:GROUP_W].astype(F32)
        gte = y * (z * _sigmoid(z))
        ms = jnp.mean(gte * gte, axis=-1, keepdims=True)
        y_ref[pl.ds(r0, lc), :] = (gte * lax.rsqrt(ms + NORM_EPS) * nw_ref[...]).astype(BF16)

    def phase_c(t, carry):
        for u in range(6):
            chunk_c(6 * t + u)
        return carry

    lax.fori_loop(0, n_chunks // 6, phase_c, 0)


def _ssd_call(pb, pdt, cw, cb, dtb, alog, dsk, nw):
    b = pb.shape[0]
    n_chunks = TT // SSD_CHUNK
    return pl.pallas_call(
        _ssd_kernel,
        grid=(b,),
        in_specs=[
            pl.BlockSpec((None, TT, PB_W), lambda bi: (bi, 0, 0)),
            pl.BlockSpec((None, TT, PDT_W), lambda bi: (bi, 0, 0)),
            _resident((3, SSM_CONV_CH)),
            _resident((1, SSM_CONV_CH)),
            _resident((n_chunks * 8, SSD_CHUNK)),
            _resident((n_chunks * 8, SSD_CHUNK)),
            _resident((1, GROUP_W)),
            _resident((1, GROUP_W)),
        ],
        out_specs=pl.BlockSpec((None, TT, GROUP_W), lambda bi: (bi, 0, 0)),
        out_shape=jax.ShapeDtypeStruct((b, TT, GROUP_W), BF16),
        scratch_shapes=[
            pltpu.VMEM((TT, SSM_CONV_CH), F32),
            pltpu.VMEM((n_chunks * 4, LANES, SSM_STATE), F32),
            pltpu.VMEM((n_chunks * 2, LANES, SSD_CHUNK), F32),
            pltpu.VMEM((n_chunks, SSD_CHUNK, LANES), F32),
            pltpu.VMEM((n_chunks * 8, SSD_CHUNK), F32),
            pltpu.VMEM((n_chunks * 8, SSD_CHUNK), F32),
            pltpu.VMEM((n_chunks * 8, SSD_CHUNK), F32),
            pltpu.VMEM((n_chunks * 8, SSD_CHUNK), F32),
            pltpu.VMEM((n_chunks * 8, SSD_CHUNK), F32),
            pltpu.VMEM((4, LANES, SSM_STATE), F32),
        ],
        compiler_params=_params(48, 1),
        name="ssd",
    )(pb, pdt, cw, cb, dtb, alog, dsk, nw)


def _ffn_kernel(*refs, tm, final_norm):
    y_refs = refs[0:12]
    x_ref, xp_ref, xn_ref, m_ref, wo_ref, nw_ref, wa_ref, wg_ref, cw_ref, cb_ref, wd_ref, fw_ref = refs[12:24]
    o_ref, u_s = refs[24:26]
    i = pl.program_id(1)
    n_i = pl.num_programs(1)
    hl = FFN_HALO
    acc = None
    for j in range(4):
        ycat = jnp.concatenate([r[...] for r in y_refs[3 * j:3 * j + 3]], axis=0)
        d = _dot(ycat, wo_ref[j * GROUP_W:(j + 1) * GROUP_W, :])
        acc = d if acc is None else acc + d
    x1 = jnp.concatenate([x_ref[...], xp_ref[...], xn_ref[...]], axis=0) + m_ref[2:3, :] * acc
    ms = jnp.mean(x1 * x1, axis=-1, keepdims=True)
    y = x1 * lax.rsqrt(ms + NORM_EPS) * nw_ref[...]
    h = (y * (1.0 + m_ref[4:5, :]) + m_ref[3:4, :]).astype(BF16)
    h_mid = h[0:tm, :]
    rid = lax.broadcasted_iota(jnp.int32, (tm, FFN_CHUNK), 0)
    has_prev = i > 0
    has_next = i < n_i - 1

    for j in range(D_FF // FFN_CHUNK):
        c0 = j * FFN_CHUNK
        a = _dot(h_mid, wa_ref[:, pl.ds(c0, FFN_CHUNK)])
        g = _dot(h, wg_ref[:, pl.ds(c0, FFN_CHUNK)])
        g_mid = g[0:tm, :]
        row_before = jnp.where(has_prev, g[tm + hl - 1:tm + hl, :], 0.0)
        row_after = jnp.where(has_next, g[tm + hl:tm + hl + 1, :], 0.0)
        g_prev = jnp.where(rid == 0, row_before, pltpu.roll(g_mid, 1, 0))
        g_next = jnp.where(rid == tm - 1, row_after, pltpu.roll(g_mid, tm - 1, 0))
        cw = cw_ref[:, pl.ds(c0, FFN_CHUNK)]
        gc = cw[0:1, :] * g_prev + cw[1:2, :] * g_mid + cw[2:3, :] * g_next + cb_ref[:, pl.ds(c0, FFN_CHUNK)]
        u_s[:, pl.ds(c0, FFN_CHUNK)] = (a * (gc * _sigmoid(gc))).astype(BF16)

    out = x1[0:tm, :] + m_ref[5:6, :] * _dot(u_s[...], wd_ref[...])
    if final_norm:
        ms2 = jnp.mean(out * out, axis=-1, keepdims=True)
        out = out * lax.rsqrt(ms2 + NORM_EPS) * fw_ref[...]
    o_ref[...] = out


def _ffn_call(ys, x, mods, layer, wo, norm_w, w_up, cw, cb, wd, fw, tm, is_ctx, final_norm):
    b, t, _ = x.shape
    n_tiles = t // tm
    hl = FFN_HALO
    hb = tm // hl
    row0 = SEQ if is_ctx else 0
    mi = 1 if is_ctx else 0

    def triple(width, rows, base):
        off_m = base // tm
        off_h = base // hl
        last = rows // hl - 1
        return [
            pl.BlockSpec((None, tm, width), lambda bi, i: (bi, i + off_m, 0)),
            pl.BlockSpec((None, hl, width), lambda bi, i: (bi, jnp.maximum(i * hb + off_h - 1, 0), 0)),
            pl.BlockSpec((None, hl, width), lambda bi, i: (bi, jnp.minimum((i + 1) * hb + off_h, last), 0)),
        ]

    in_specs = []
    args = []
    for yv in ys:
        in_specs += triple(GROUP_W, TT, row0)
        args += [yv, yv, yv]
    in_specs += triple(D_MODEL, t, 0)
    args += [x, x, x]
    in_specs += [
        pl.BlockSpec((None, None, 8, D_MODEL), lambda bi, i: (bi, mi, 0, 0)),
        _resident((None, 4 * GROUP_W, D_MODEL), (layer, 0, 0)),
        _resident((1, D_MODEL)),
        _resident((None, D_MODEL, D_FF), (layer, 0, 0)),
        _resident((None, D_MODEL, D_FF), (layer, 0, 1)),
        _resident((None, 3, D_FF), (layer, 0, 0)),
        _resident((None, 1, D_FF), (layer, 0, 0)),
        _resident((None, D_FF, D_MODEL), (layer, 0, 0)),
        _resident((1, D_MODEL)),
    ]
    args += [mods, wo, norm_w, w_up, w_up, cw, cb, wd, fw]
    return pl.pallas_call(
        functools.partial(_ffn_kernel, tm=tm, final_norm=final_norm),
        grid=(b, n_tiles),
        in_specs=in_specs,
        out_specs=pl.BlockSpec((None, tm, D_MODEL), lambda bi, i: (bi, i, 0)),
        out_shape=jax.ShapeDtypeStruct((b, t, D_MODEL), F32),
        scratch_shapes=[pltpu.VMEM((tm, D_FF), BF16)],
        compiler_params=_params(60),
        name="ffn_ctx" if is_ctx else "ffn_lat",
    )(*args)


def _pad_cols(w, width):
    return jnp.pad(w, ((0, 0), (0, width - w.shape[1])))


def _proj_weight(w_in):
    a_cols = 512
    b_cols = GROUP_W + SSM_CONV_CH + 8
    c_cols = 512
    w_in = w_in.astype(BF16)
    wa = w_in[..., :a_cols]
    wb = w_in[..., a_cols:a_cols + b_cols]
    wc = w_in[..., a_cols + b_cols:a_cols + b_cols + c_cols]
    wd = w_in[..., a_cols + b_cols + c_cols:]
    zeros = lambda n: jnp.zeros(w_in.shape[:-1] + (n,), w_in.dtype)
    w_zx = wb[..., :GROUP_W + SSM_CONV_CH]
    w_dt = jnp.concatenate([wb[..., GROUP_W + SSM_CONV_CH:], zeros(PDT_W - 8)], axis=-1)
    w_cq = jnp.concatenate([wd[..., :MLA_Q_RANK], zeros(2 * LANES - MLA_Q_RANK)], axis=-1)
    w_ckv = wd[..., MLA_Q_RANK:MLA_Q_RANK + MLA_KV_RANK]
    w_kr = jnp.concatenate([zeros(MLA_NOPE), wd[..., MLA_Q_RANK + MLA_KV_RANK:],
                            zeros(LANES - MLA_NOPE - MLA_ROPE)], axis=-1)
    return jnp.concatenate([wa, w_zx, w_dt, wc, w_cq, w_ckv, w_kr], axis=-1)


def _mla_weights(w_uq_l, w_ukv_l):
    dq = MLA_NOPE + MLA_ROPE
    wq = w_uq_l.reshape(MLA_Q_RANK, 4, dq)
    wq = jnp.pad(wq, ((0, 2 * LANES - MLA_Q_RANK), (0, 0), (0, LANES - dq))).reshape(2 * LANES, 4 * LANES)
    wkv = w_ukv_l.reshape(MLA_KV_RANK, 4, MLA_NOPE + HEAD_DIM)
    wk = jnp.pad(wkv[:, :, :MLA_NOPE], ((0, 0), (0, 0), (0, LANES - MLA_NOPE))).reshape(MLA_KV_RANK, 4 * LANES)
    wv = wkv[:, :, MLA_NOPE:].reshape(MLA_KV_RANK, 4 * HEAD_DIM)
    return wq.astype(BF16), jnp.concatenate([wk, wv], axis=1).astype(BF16)


def kernel(x, c, ctx, c_ctx, norm1_w, w_mod, b_mod, w_in, attn_q_norm, attn_k_norm, ssm_conv_w, ssm_conv_b,
           ssm_dt_bias, ssm_a_log, ssm_d, ssm_norm_w, win_sink, mla_q_norm, mla_w_uq, mla_kv_norm, mla_w_ukv,
           w_out, norm2_w, ffn_w_up, ffn_conv_w, ffn_conv_b, ffn_w_down, final_norm_w):
    bsz = x.shape[0]
    depth = w_in.shape[0]
    assert x.shape[1:] == (SEQ, D_MODEL) and ctx.shape[1:] == (CTX_LEN, D_MODEL)

    tables = _axial_tables(HEAD_DIM, 0, 2) + _axial_tables(MLA_ROPE, MLA_NOPE, 1)

    n_rows = ((bsz + 1 + 7) // 8) * 8
    cvec = jnp.concatenate([c, c_ctx[None, :], jnp.zeros((n_rows - bsz - 1, D_MODEL), F32)], axis=0)
    mods_all = _mod_call(cvec, w_mod, b_mod.reshape(depth, 1, 6 * D_MODEL))

    w_in_all = _proj_weight(w_in)
    w_o_all = w_out.astype(BF16)
    w_up_all = ffn_w_up.astype(BF16)
    wd_all = ffn_w_down.astype(BF16)
    cw_all = jnp.swapaxes(ffn_conv_w, 1, 2)
    cb_all = ffn_conv_b.reshape(depth, 1, D_FF)
    fw = final_norm_w.reshape(1, D_MODEL)

    x_lat, x_ctx = x, ctx
    for l in range(depth):
        with_ctx = l < depth - 1
        m_lat = mods_all[l, :bsz].reshape(bsz, 6, D_MODEL)
        m_ctx = jnp.broadcast_to(mods_all[l, bsz].reshape(1, 6, D_MODEL), (bsz, 6, D_MODEL))
        mods = jnp.pad(jnp.stack([m_lat, m_ctx], axis=1), ((0, 0), (0, 0), (0, 2), (0, 0)))

        wq, wkv = _mla_weights(mla_w_uq[l], mla_w_ukv[l])
        qa, ka, va, pb, pdt, qc, kc, vc, qd, kd, vd = _proj_call(
            x_lat, x_ctx, norm1_w[l].reshape(1, D_MODEL), mods, l, w_in_all, tables,
            jnp.tile(attn_q_norm[l], 2).reshape(1, LANES), jnp.tile(attn_k_norm[l], 2).reshape(1, LANES),
            _pad_cols(mla_q_norm[l].reshape(1, MLA_Q_RANK), 2 * LANES), mla_kv_norm[l].reshape(1, MLA_KV_RANK),
            wq, wkv)

        ya = _dense_attn_call(qa, ka, va, None, (0, 0, 0, 0), with_ctx, "attn_gqa")
        ym = _dense_attn_call(qd, kd, vd, None, (0, 1, 2, 3), with_ctx, "attn_mla")
        yw = _win_attn_call(qc, kc, vc, win_sink[l], with_ctx)
        yb = _ssd_call(
            pb, pdt, ssm_conv_w[l].T, ssm_conv_b[l].reshape(1, SSM_CONV_CH),
            jnp.tile(jnp.broadcast_to(ssm_dt_bias[l].reshape(8, 1), (8, SSD_CHUNK)), (TT // SSD_CHUNK, 1)),
            jnp.tile(jnp.broadcast_to(ssm_a_log[l].reshape(8, 1), (8, SSD_CHUNK)), (TT // SSD_CHUNK, 1)),
            jnp.repeat(ssm_d[l], HEAD_DIM).reshape(1, GROUP_W), ssm_norm_w[l].reshape(1, GROUP_W))

        ys = (ya, yb, yw, ym)
        n2 = norm2_w[l].reshape(1, D_MODEL)
        last = l == depth - 1
        ffn_w = (w_o_all, n2, w_up_all, cw_all, cb_all, wd_all, fw)
        x_lat_next = _ffn_call(ys, x_lat, mods, l, *ffn_w, FFN_TILE, False, last)
        if with_ctx:
            x_ctx = _ffn_call(ys, x_ctx, mods, l, *ffn_w, CTX_LEN, True, False)
        x_lat = x_lat_next
    return x_lat
```

```python
import functools

import numpy as np
import jax
import jax.numpy as jnp
from jax import lax
from jax.experimental import pallas as pl
from jax.experimental.pallas import tpu as pltpu

F32 = jnp.float32
BF16 = jnp.bfloat16

D_MODEL = 1024
SEQ = 2048
CTX_LEN = 256
TT = SEQ + CTX_LEN
GRID_W = 64
ROPE_THETA = 10000.0
NORM_EPS = 1e-6
HEAD_DIM = 64
GROUP_W = 256
WINDOW = 128
SSM_STATE = 128
SSM_CONV_CH = 768
MLA_NOPE = 64
MLA_ROPE = 32
MLA_Q_RANK = 192
MLA_KV_RANK = 128
D_FF = 2816

LANES = 128
BF16_ROWS = 16
TOK_TILE = 256
N_LAT_TILES = SEQ // TOK_TILE
N_TILES = TT // TOK_TILE
SSD_CHUNK = 128
FFN_TILE = 1024
FFN_CHUNK = 256
FFN_HALO = 16
LOG2E = 1.4426950408889634

PA_W, PB_W, PDT_W, PC_W, PD_W = 512, 1024, 128, 512, 512
OFF_A = 0
OFF_B = OFF_A + PA_W
OFF_DT = OFF_B + PB_W
OFF_C = OFF_DT + PDT_W
OFF_D = OFF_C + PC_W
PROJ_W = OFF_D + PD_W


def _resident(shape, index=None):
    index = (0,) * len(shape) if index is None else tuple(index)
    return pl.BlockSpec(shape, lambda *_: index, pipeline_mode=pl.Buffered(1))


def _params(vmem_mb, ndims=2):
    return pltpu.CompilerParams(
        dimension_semantics=("arbitrary",) * ndims,
        vmem_limit_bytes=vmem_mb * 1024 * 1024,
    )


def _sigmoid(v):
    return 1.0 / (1.0 + jnp.exp(-v))


def _dot(a, b):
    return jnp.dot(a, b, preferred_element_type=F32)


def _dot_nt(a, b):
    return lax.dot_general(a, b, (((1,), (1,)), ((), ())), preferred_element_type=F32)


def _mod_kernel(c_ref, w_ref, b_ref, o_ref):
    a = c_ref[...]
    s = a * _sigmoid(a)
    w = w_ref[...]
    s_hi = s.astype(BF16)
    s_lo = (s - s_hi.astype(F32)).astype(BF16)
    w_hi = w.astype(BF16)
    w_lo = (w - w_hi.astype(F32)).astype(BF16)
    o_ref[...] = _dot(s_hi, w_hi) + _dot(s_hi, w_lo) + _dot(s_lo, w_hi) + b_ref[...]


def _mod_call(cvec, w_mod, b_mod):
    n_layers, d, n6 = w_mod.shape
    rows = cvec.shape[0]
    tn = 1024
    return pl.pallas_call(
        _mod_kernel,
        grid=(n_layers, n6 // tn),
        in_specs=[
            pl.BlockSpec((rows, d), lambda l, j: (0, 0)),
            pl.BlockSpec((None, d, tn), lambda l, j: (l, 0, j)),
            pl.BlockSpec((None, 1, tn), lambda l, j: (l, 0, j)),
        ],
        out_specs=pl.BlockSpec((None, rows, tn), lambda l, j: (l, 0, j)),
        out_shape=jax.ShapeDtypeStruct((n_layers, rows, n6), F32),
        compiler_params=_params(40),
        name="mod",
    )(cvec, w_mod, b_mod)


def _rope(x, cos, sin_signed, half):
    lane = lax.broadcasted_iota(jnp.int32, x.shape, 1)
    first = (lane & (2 * half - 1)) < half
    partner = jnp.where(first, pltpu.roll(x, LANES - half, 1), pltpu.roll(x, half, 1))
    return x * cos + partner * sin_signed


def _axial_tables(rot_dim, lane_lo, reps):
    half = rot_dim // 2
    t = np.arange(SEQ)
    row = (t // GRID_W).astype(np.float32)
    col = (t % GRID_W).astype(np.float32)
    inv_freq = np.power(np.float32(ROPE_THETA), -np.arange(0, half, 2, dtype=np.float32) / np.float32(half))
    inv_freq = inv_freq.astype(np.float32)
    ang_r = (row[:, None] * inv_freq[None, :]).astype(np.float32)
    ang_c = (col[:, None] * inv_freq[None, :]).astype(np.float32)
    ang = np.concatenate([ang_r, ang_r, ang_c, ang_c], axis=-1).astype(np.float64)
    cos = np.cos(ang)
    sin = np.sin(ang)
    quarter = half // 2
    sign = np.where((np.arange(rot_dim) % half) < quarter, -1.0, 1.0)
    cos_t = np.ones((TT, LANES), np.float64)
    sin_t = np.zeros((TT, LANES), np.float64)
    for r in range(reps):
        lo = lane_lo + r * rot_dim
        cos_t[:SEQ, lo:lo + rot_dim] = cos
        sin_t[:SEQ, lo:lo + rot_dim] = sin * sign[None, :]
    return jnp.asarray(cos_t, F32), jnp.asarray(sin_t, F32)


def _store_value_slabs(pair, swapped, lo, v_ref):
    v_ref[0] = jnp.where(lo, pair, 1.0).astype(BF16)
    v_ref[1] = jnp.where(lo, 1.0, swapped).astype(BF16)
    v_ref[2] = jnp.where(lo, swapped, 1.0).astype(BF16)
    v_ref[3] = jnp.where(lo, 1.0, pair).astype(BF16)


def _gqa_prep(p, cos, sin, qn, kn, do_norm, q_ref, k_ref, v_ref):
    lane = lax.broadcasted_iota(jnp.int32, (p.shape[0], LANES), 1)
    lo = lane < HEAD_DIM

    def head_norm(v, gain):
        v2 = v * v
        s_lo = jnp.sum(jnp.where(lo, v2, 0.0), axis=-1, keepdims=True)
        s_hi = jnp.sum(jnp.where(lo, 0.0, v2), axis=-1, keepdims=True)
        ms = jnp.where(lo, s_lo, s_hi) * (1.0 / HEAD_DIM)
        return v * lax.rsqrt(ms + NORM_EPS) * gain

    scale = HEAD_DIM ** -0.5 * LOG2E
    for j in range(2):
        v = p[:, j * LANES:(j + 1) * LANES]
        if do_norm:
            v = head_norm(v, qn)
        v = _rope(v, cos, sin, HEAD_DIM // 4) * scale
        sw = pltpu.roll(v, HEAD_DIM, 1)
        if j == 0:
            q_ref[:, 0:LANES] = jnp.where(lo, v, 0.0).astype(BF16)
            q_ref[:, LANES:2 * LANES] = jnp.where(lo, sw, 0.0).astype(BF16)
        else:
            q_ref[:, 2 * LANES:3 * LANES] = jnp.where(lo, 0.0, sw).astype(BF16)
            q_ref[:, 3 * LANES:4 * LANES] = jnp.where(lo, 0.0, v).astype(BF16)

    k = p[:, 2 * LANES:3 * LANES]
    if do_norm:
        k = head_norm(k, kn)
    k_ref[0] = _rope(k, cos, sin, HEAD_DIM // 4).astype(BF16)

    vv = p[:, 3 * LANES:4 * LANES]
    _store_value_slabs(vv, pltpu.roll(vv, HEAD_DIM, 1), lo, v_ref)


def _mla_prep(p, cos, sin, gq, gkv, wq_ref, wkv_ref, q_ref, k_ref, v_ref):
    half = MLA_ROPE // 4
    cq = p[:, 0:2 * LANES]
    ms = jnp.sum(cq * cq, axis=-1, keepdims=True) * (1.0 / MLA_Q_RANK)
    cqn = (cq * lax.rsqrt(ms + NORM_EPS) * gq).astype(BF16)
    q = _dot(cqn, wq_ref[...])
    scale = (MLA_NOPE + MLA_ROPE) ** -0.5 * LOG2E
    for h in range(4):
        qh = _rope(q[:, h * LANES:(h + 1) * LANES], cos, sin, half) * scale
        q_ref[:, h * LANES:(h + 1) * LANES] = qh.astype(BF16)

    ckv = p[:, 2 * LANES:3 * LANES]
    ms = jnp.mean(ckv * ckv, axis=-1, keepdims=True)
    ckvn = (ckv * lax.rsqrt(ms + NORM_EPS) * gkv).astype(BF16)
    kv = _dot(ckvn, wkv_ref[...])
    kr = _rope(p[:, 3 * LANES:4 * LANES], cos, sin, half)
    for h in range(4):
        k_ref[h] = (kv[:, h * LANES:(h + 1) * LANES] + kr).astype(BF16)
    lane = lax.broadcasted_iota(jnp.int32, (p.shape[0], LANES), 1)
    lo = lane < HEAD_DIM
    v01 = kv[:, 4 * LANES:5 * LANES]
    v23 = kv[:, 5 * LANES:6 * LANES]
    v_ref[0] = jnp.where(lo, v01, 1.0).astype(BF16)
    v_ref[1] = jnp.where(lo, 1.0, v01).astype(BF16)
    v_ref[2] = jnp.where(lo, v23, 1.0).astype(BF16)
    v_ref[3] = jnp.where(lo, 1.0, v23).astype(BF16)


PROJ_SUB = 3
PROJ_STEPS = N_TILES // PROJ_SUB


def _proj_kernel(xl0_ref, xl1_ref, xl2_ref, xc_ref, nw_ref, m_ref, w_ref, cosa_ref, sina_ref, cosm_ref, sinm_ref,
                 qn_ref, kn_ref, gq_ref, gkv_ref, wq_ref, wkv_ref,
                 qa_ref, ka_ref, va_ref, pb_ref, pdt_ref, qc_ref, kc_ref, vc_ref, qd_ref, kd_ref, vd_ref):
    last_step = pl.program_id(1) == PROJ_STEPS - 1
    x_refs = (xl0_ref, xl1_ref, xl2_ref)
    for u in range(PROJ_SUB):
        rows = pl.ds(u * TOK_TILE, TOK_TILE)
        if u == PROJ_SUB - 1:
            x = jnp.where(last_step, xc_ref[...], x_refs[u][...])
            m = jnp.where(last_step, m_ref[1], m_ref[0])
        else:
            x = x_refs[u][...]
            m = m_ref[0]
        ms = jnp.mean(x * x, axis=-1, keepdims=True)
        y = x * lax.rsqrt(ms + NORM_EPS) * nw_ref[...]
        h = (y * (1.0 + m[1:2, :]) + m[0:1, :]).astype(BF16)
        cosa = cosa_ref[rows, :]
        sina = sina_ref[rows, :]
        _mla_prep(_dot(h, w_ref[:, OFF_D:OFF_D + PD_W]), cosm_ref[rows, :], sinm_ref[rows, :], gq_ref[...],
                  gkv_ref[...], wq_ref, wkv_ref, qd_ref.at[rows, :], kd_ref.at[:, rows, :], vd_ref.at[:, rows, :])
        _gqa_prep(_dot(h, w_ref[:, OFF_A:OFF_A + PA_W]), cosa, sina, qn_ref[...], kn_ref[...], True,
                  qa_ref.at[rows, :], ka_ref.at[:, rows, :], va_ref.at[:, rows, :])
        _gqa_prep(_dot(h, w_ref[:, OFF_C:OFF_C + PC_W]), cosa, sina, None, None, False,
                  qc_ref.at[rows, :], kc_ref.at[:, rows, :], vc_ref.at[:, rows, :])
        pdt_ref[rows, :] = _dot(h, w_ref[:, OFF_DT:OFF_DT + PDT_W])
        pb_ref[rows, :] = _dot(h, w_ref[:, OFF_B:OFF_B + PB_W]).astype(BF16)


def _proj_call(x_lat, x_ctx, norm_w, mods, layer, w, tables, qn, kn, gq, gkv, wq, wkv):
    b = x_lat.shape[0]
    step_rows = PROJ_SUB * TOK_TILE
    tok = lambda width: pl.BlockSpec((None, step_rows, width), lambda bi, i: (bi, i, 0))
    slab = lambda n: pl.BlockSpec((None, n, step_rows, LANES), lambda bi, i: (bi, 0, i, 0))
    table = pl.BlockSpec((step_rows, LANES), lambda bi, i: (i, 0))
    x_tile = lambda u: pl.BlockSpec(
        (None, TOK_TILE, D_MODEL), lambda bi, i: (bi, jnp.minimum(i * PROJ_SUB + u, N_LAT_TILES - 1), 0))
    q_shape = jax.ShapeDtypeStruct((b, TT, 4 * LANES), BF16)
    slab_shape = lambda n: jax.ShapeDtypeStruct((b, n, TT, LANES), BF16)
    return pl.pallas_call(
        _proj_kernel,
        grid=(b, PROJ_STEPS),
        in_specs=[
            x_tile(0), x_tile(1), x_tile(2),
            pl.BlockSpec((None, TOK_TILE, D_MODEL), lambda bi, i: (bi, 0, 0)),
            _resident((1, D_MODEL)),
            pl.BlockSpec((None, 2, 8, D_MODEL), lambda bi, i: (bi, 0, 0, 0)),
            _resident((None, D_MODEL, PROJ_W), (layer, 0, 0)),
            table, table, table, table,
            _resident((1, LANES)), _resident((1, LANES)),
            _resident((1, 2 * LANES)), _resident((1, LANES)),
            _resident((2 * LANES, 4 * LANES)), _resident((LANES, 6 * LANES)),
        ],
        out_specs=[tok(4 * LANES), slab(1), slab(4), tok(PB_W), tok(PDT_W),
                   tok(4 * LANES), slab(1), slab(4), tok(4 * LANES), slab(4), slab(4)],
        out_shape=[q_shape, slab_shape(1), slab_shape(4),
                   jax.ShapeDtypeStruct((b, TT, PB_W), BF16), jax.ShapeDtypeStruct((b, TT, PDT_W), F32),
                   q_shape, slab_shape(1), slab_shape(4), q_shape, slab_shape(4), slab_shape(4)],
        compiler_params=_params(48),
        name="proj",
    )(x_lat, x_lat, x_lat, x_ctx, norm_w, mods, w, *tables, qn, kn, gq, gkv, wq, wkv)


def _softmax_pv(s_parts, v_parts, sink):
    m = s_parts[0].max(axis=-1, keepdims=True)
    for s in s_parts[1:]:
        m = jnp.maximum(m, s.max(axis=-1, keepdims=True))
    if sink is not None:
        m = jnp.maximum(m, sink)
    acc = None
    for s, v in zip(s_parts, v_parts):
        o = _dot(jnp.exp2(s - m).astype(BF16), v)
        acc = o if acc is None else acc + o
    den = pltpu.roll(acc, HEAD_DIM, 1)
    if sink is not None:
        den = den + jnp.exp2(sink - m)
    return acc / den


def _merge_heads(outs, store):
    lane = lax.broadcasted_iota(jnp.int32, outs[0].shape, 1)
    lo = lane < HEAD_DIM
    store(0, jnp.where(lo, outs[0], outs[1]).astype(BF16))
    store(1, jnp.where(lo, outs[2], outs[3]).astype(BF16))


ATT_TILE = 256
DENSE_TILE = 512


def _dense_attn_kernel(*refs, k_of, has_sink, with_ctx):
    if has_sink:
        sink_ref, q_ref, k_ref, v_ref, o_ref = refs
    else:
        q_ref, k_ref, v_ref, o_ref = refs
        sink_ref = None

    def tile(r0, rows, k_lo, k_hi):
        def scores(h):
            return _dot_nt(q_ref[pl.ds(r0, rows), h * LANES:(h + 1) * LANES], k_ref[k_of[h], k_lo:k_hi, :])

        def store(j, val):
            o_ref[pl.ds(r0, rows), j * LANES:(j + 1) * LANES] = val

        outs = []
        s_next = scores(0)
        for h in range(4):
            s = s_next
            if h < 3:
                s_next = scores(h + 1)
            sink = sink_ref[h] * LOG2E if has_sink else None
            outs.append(_softmax_pv([s], [v_ref[h, k_lo:k_hi, :]], sink))
        _merge_heads(outs, store)

    for t in range(SEQ // DENSE_TILE):
        tile(t * DENSE_TILE, DENSE_TILE, 0, TT)
    if with_ctx:
        tile(SEQ, CTX_LEN, SEQ, TT)
    else:
        o_ref[SEQ:TT, :] = jnp.zeros((CTX_LEN, GROUP_W), BF16)


def _dense_attn_call(q, k, v, sink, k_of, with_ctx, name):
    b = q.shape[0]
    n_k = k.shape[1]
    has_sink = sink is not None
    in_specs = [
        pl.BlockSpec((None, TT, 4 * LANES), lambda bi: (bi, 0, 0)),
        pl.BlockSpec((None, n_k, TT, LANES), lambda bi: (bi, 0, 0, 0)),
        pl.BlockSpec((None, 4, TT, LANES), lambda bi: (bi, 0, 0, 0)),
    ]
    args = [q, k, v]
    if has_sink:
        in_specs = [pl.BlockSpec(memory_space=pltpu.SMEM)] + in_specs
        args = [sink] + args
    return pl.pallas_call(
        functools.partial(_dense_attn_kernel, k_of=k_of, has_sink=has_sink, with_ctx=with_ctx),
        grid=(b,),
        in_specs=in_specs,
        out_specs=pl.BlockSpec((None, TT, GROUP_W), lambda bi: (bi, 0, 0)),
        out_shape=jax.ShapeDtypeStruct((b, TT, GROUP_W), BF16),
        compiler_params=_params(48, 1),
        name=name,
    )(*args)


WIN_BAND = ATT_TILE + 2 * WINDOW


def _win_attn_kernel(sink_ref, q_ref, k_ref, v_ref, o_ref, *, with_ctx):
    k_ctx = k_ref[0, SEQ:TT, :]

    def finish(r0, rows, outs):
        def store(j, val):
            o_ref[pl.ds(r0, rows), j * LANES:(j + 1) * LANES] = val
        _merge_heads(outs, store)

    def band_tile(t):
        r0 = pl.multiple_of(t * ATT_TILE, ATT_TILE)
        start = pl.multiple_of(jnp.clip(r0 - WINDOW, 0, SEQ - WIN_BAND), WINDOW)
        qpos = r0 + lax.broadcasted_iota(jnp.int32, (ATT_TILE, WIN_BAND), 0)
        kpos = start + lax.broadcasted_iota(jnp.int32, (ATT_TILE, WIN_BAND), 1)
        mask = jnp.abs(kpos - qpos) <= WINDOW
        k_band = k_ref[0, pl.ds(start, WIN_BAND), :]

        def scores(h):
            qh = q_ref[pl.ds(r0, ATT_TILE), h * LANES:(h + 1) * LANES]
            return [jnp.where(mask, _dot_nt(qh, k_band), -jnp.inf), _dot_nt(qh, k_ctx)]

        outs = []
        s_next = scores(0)
        for h in range(4):
            s_parts = s_next
            if h < 3:
                s_next = scores(h + 1)
            outs.append(_softmax_pv(
                s_parts,
                [v_ref[h, pl.ds(start, WIN_BAND), :], v_ref[h, SEQ:TT, :]],
                sink_ref[h] * LOG2E))
        finish(r0, ATT_TILE, outs)

    def body(t, carry):
        for u in range(4):
            band_tile(4 * t + u)
        return carry

    lax.fori_loop(0, SEQ // ATT_TILE // 4, body, 0)
    if with_ctx:
        outs = []
        for h in range(4):
            qh = q_ref[SEQ:TT, h * LANES:(h + 1) * LANES]
            outs.append(_softmax_pv([_dot_nt(qh, k_ctx)], [v_ref[h, SEQ:TT, :]], sink_ref[h] * LOG2E))
        finish(SEQ, CTX_LEN, outs)
    else:
        o_ref[SEQ:TT, :] = jnp.zeros((CTX_LEN, GROUP_W), BF16)


def _win_attn_call(q, k, v, sink, with_ctx):
    b = q.shape[0]
    return pl.pallas_call(
        functools.partial(_win_attn_kernel, with_ctx=with_ctx),
        grid=(b,),
        in_specs=[
            pl.BlockSpec(memory_space=pltpu.SMEM),
            pl.BlockSpec((None, TT, 4 * LANES), lambda bi: (bi, 0, 0)),
            pl.BlockSpec((None, 1, TT, LANES), lambda bi: (bi, 0, 0, 0)),
            pl.BlockSpec((None, 4, TT, LANES), lambda bi: (bi, 0, 0, 0)),
        ],
        out_specs=pl.BlockSpec((None, TT, GROUP_W), lambda bi: (bi, 0, 0)),
        out_shape=jax.ShapeDtypeStruct((b, TT, GROUP_W), BF16),
        compiler_params=_params(32, 1),
        name="attn_win",
    )(sink, q, k, v)


def _ssd_kernel(pb_ref, dt_ref, cw_ref, cb_ref, dtb_ref, alog_ref, dsk_ref, nw_ref, y_ref,
                xbc_s, st_s, yd_s, csc_s, dt_s, cs_s, w_s, ecs_s, etot_s, h_s):
    lc = SSD_CHUNK
    n_chunks = TT // lc
    n_lat = SEQ // lc
    n_ctx = CTX_LEN // lc
    row_i = lax.broadcasted_iota(jnp.int32, (lc, lc), 0)
    col_i = lax.broadcasted_iota(jnp.int32, (lc, lc), 1)
    top = row_i < HEAD_DIM
    keep_f = row_i <= col_i
    keep_b = row_i >= col_i

    def pair_rows(tab, j0):
        return jnp.where(top, tab[j0:j0 + 1, :], tab[j0 + 1:j0 + 2, :])

    pre = jnp.concatenate([dt_ref[c * lc:(c + 1) * lc, :].T[0:8, :] for c in range(n_chunks)], axis=0) + dtb_ref[...]
    dt_all = jnp.maximum(pre, 0.0) + jnp.log1p(jnp.exp(-jnp.abs(pre)))
    da = dt_all * (-jnp.exp(alog_ref[...]))
    lane_t = lax.broadcasted_iota(jnp.int32, da.shape, 1)
    row_t = lax.broadcasted_iota(jnp.int32, da.shape, 0)
    cs = da
    k = 1
    while k < lc:
        cs = cs + jnp.where(lane_t >= k, pltpu.roll(cs, k, 1), 0.0)
        k *= 2
    tot = cs[:, lc - 1:lc]
    cs_all = jnp.where((row_t & 7) < 4, cs, tot - cs + da)
    dt_s[...] = dt_all
    cs_s[...] = cs_all
    w_s[...] = jnp.exp(tot - cs_all)
    ecs_s[...] = jnp.exp(cs_all)
    etot_s[...] = jnp.broadcast_to(jnp.exp(tot), da.shape)
    for c in range(n_chunks):
        csc_s[c] = jnp.concatenate([cs_all[c * 8:(c + 1) * 8, :]] * (lc // 8), axis=0).T

    def chunk_a(c):
        r0 = pl.multiple_of(c * lc, lc)
        t0 = pl.multiple_of(c * 8, 8)
        first = jnp.logical_or(c == 0, c == n_lat)
        last = jnp.logical_or(c == n_lat - 1, c == n_chunks - 1)
        u = pb_ref[pl.ds(r0, lc), GROUP_W:GROUP_W + SSM_CONV_CH].astype(F32)
        rp = pl.multiple_of(jnp.maximum(r0 - BF16_ROWS, 0), BF16_ROWS)
        rn = pl.multiple_of(jnp.minimum(r0 + lc, TT - BF16_ROWS), BF16_ROWS)
        prev_blk = pb_ref[pl.ds(rp, BF16_ROWS), GROUP_W:GROUP_W + SSM_CONV_CH].astype(F32)
        next_blk = pb_ref[pl.ds(rn, BF16_ROWS), GROUP_W:GROUP_W + SSM_CONV_CH].astype(F32)
        prev_row = jnp.where(first, 0.0, prev_blk[BF16_ROWS - 1:BF16_ROWS, :])
        next_row = jnp.where(last, 0.0, next_blk[0:1, :])
        rows = lax.broadcasted_iota(jnp.int32, (lc, SSM_CONV_CH), 0)
        up = jnp.where(rows == 0, prev_row, pltpu.roll(u, 1, 0))
        un = jnp.where(rows == lc - 1, next_row, pltpu.roll(u, lc - 1, 0))
        v = cw_ref[0:1, :] * up + cw_ref[1:2, :] * u + cw_ref[2:3, :] * un + cb_ref[...]
        act = v * _sigmoid(v)
        xbc_s[pl.ds(r0, lc), :] = act
        xs = act[:, 0:GROUP_W]
        bm = act[:, GROUP_W:2 * GROUP_W]
        cm = act[:, 2 * GROUP_W:3 * GROUP_W]
        dt_t = dt_s[pl.ds(t0, 8), :]
        cs_t = cs_s[pl.ds(t0, 8), :]
        w_t = w_s[pl.ds(t0, 8), :]
        cs_cols = csc_s[c]

        for g in range(2):
            bm_g = bm[:, g * LANES:(g + 1) * LANES].astype(BF16)
            cm_g = cm[:, g * LANES:(g + 1) * LANES].astype(BF16)
            cb_t = _dot_nt(bm_g, cm_g)
            xs_t = xs[:, g * LANES:(g + 1) * LANES].T
            yd = None
            for d in range(2):
                j0 = d * 4 + 2 * g
                keep = keep_f if d == 0 else keep_b
                xdt_t = xs_t * pair_rows(dt_t, j0)
                xdt_b = xdt_t.astype(BF16)
                parts = []
                for hh in range(2):
                    j = j0 + hh
                    seg_t = cs_t[j:j + 1, :] - cs_cols[:, j:j + 1]
                    dec_t = jnp.exp(jnp.where(keep, seg_t, -jnp.inf))
                    parts.append(_dot(xdt_b[hh * HEAD_DIM:(hh + 1) * HEAD_DIM, :], (cb_t * dec_t).astype(BF16)))
                y_dg = jnp.concatenate(parts, axis=0)
                yd = y_dg if yd is None else yd + y_dg
                xw_t = xdt_t * pair_rows(w_t, j0)
                st_s[c * 4 + d * 2 + g] = _dot(xw_t.astype(BF16), bm_g)
            yd_s[c * 2 + g] = yd

    def phase_a(t, carry):
        for u in range(3):
            chunk_a(3 * t + u)
        return carry

    lax.fori_loop(0, n_chunks // 3, phase_a, 0)

    h_s[...] = jnp.zeros_like(h_s)

    def phase_b(base_chunk, n_seg):
        def body(t, carry):
            for d in range(2):
                c = base_chunk + t if d == 0 else base_chunk + n_seg - 1 - t
                e = etot_s[pl.ds(pl.multiple_of(c * 8, 8), 8), :]
                for g in range(2):
                    slot = c * 4 + d * 2 + g
                    s_c = st_s[slot]
                    h_in = h_s[d * 2 + g]
                    st_s[slot] = h_in
                    h_s[d * 2 + g] = h_in * pair_rows(e, d * 4 + 2 * g) + s_c
            return carry
        lax.fori_loop(0, n_seg, body, 0)

    phase_b(n_lat, n_ctx)
    phase_b(0, n_lat)

    def chunk_c(c):
        r0 = pl.multiple_of(c * lc, lc)
        xs = xbc_s[pl.ds(r0, lc), 0:GROUP_W]
        cm = xbc_s[pl.ds(r0, lc), 2 * GROUP_W:3 * GROUP_W]
        ecs = ecs_s[pl.ds(pl.multiple_of(c * 8, 8), 8), :]
        halves = []
        for g in range(2):
            cm_g = cm[:, g * LANES:(g + 1) * LANES].astype(BF16)
            y_t = yd_s[c * 2 + g]
            for d in range(2):
                h_in = st_s[c * 4 + d * 2 + g].astype(BF16)
                y_t = y_t + _dot_nt(h_in, cm_g) * pair_rows(ecs, d * 4 + 2 * g)
            halves.append(y_t.T)
        y = jnp.concatenate(halves, axis=1) + xs * dsk_ref[...]
        z = pb_ref[pl.ds(r0, lc), 0:GROUP_W].astype(F32)
        gte = y * (z * _sigmoid(z))
        ms = jnp.mean(gte * gte, axis=-1, keepdims=True)
        y_ref[pl.ds(r0, lc), :] = (gte * lax.rsqrt(ms + NORM_EPS) * nw_ref[...]).astype(BF16)

    def phase_c(t, carry):
        for u in range(9):
            chunk_c(9 * t + u)
        return carry

    lax.fori_loop(0, n_chunks // 9, phase_c, 0)


def _ssd_call(pb, pdt, cw, cb, dtb, alog, dsk, nw):
    b = pb.shape[0]
    n_chunks = TT // SSD_CHUNK
    return pl.pallas_call(
        _ssd_kernel,
        grid=(b,),
        in_specs=[
            pl.BlockSpec((None, TT, PB_W), lambda bi: (bi, 0, 0)),
            pl.BlockSpec((None, TT, PDT_W), lambda bi: (bi, 0, 0)),
            _resident((3, SSM_CONV_CH)),
            _resident((1, SSM_CONV_CH)),
            _resident((n_chunks * 8, SSD_CHUNK)),
            _resident((n_chunks * 8, SSD_CHUNK)),
            _resident((1, GROUP_W)),
            _resident((1, GROUP_W)),
        ],
        out_specs=pl.BlockSpec((None, TT, GROUP_W), lambda bi: (bi, 0, 0)),
        out_shape=jax.ShapeDtypeStruct((b, TT, GROUP_W), BF16),
        scratch_shapes=[
            pltpu.VMEM((TT, SSM_CONV_CH), F32),
            pltpu.VMEM((n_chunks * 4, LANES, SSM_STATE), F32),
            pltpu.VMEM((n_chunks * 2, LANES, SSD_CHUNK), F32),
            pltpu.VMEM((n_chunks, SSD_CHUNK, LANES), F32),
            pltpu.VMEM((n_chunks * 8, SSD_CHUNK), F32),
            pltpu.VMEM((n_chunks * 8, SSD_CHUNK), F32),
            pltpu.VMEM((n_chunks * 8, SSD_CHUNK), F32),
            pltpu.VMEM((n_chunks * 8, SSD_CHUNK), F32),
            pltpu.VMEM((n_chunks * 8, SSD_CHUNK), F32),
            pltpu.VMEM((4, LANES, SSM_STATE), F32),
        ],
        compiler_params=_params(48, 1),
        name="ssd",
    )(pb, pdt, cw, cb, dtb, alog, dsk, nw)


def _ffn_kernel(*refs, tm, final_norm):
    y_refs = refs[0:12]
    x_ref, xp_ref, xn_ref, m_ref, wo_ref, nw_ref, wa_ref, wg_ref, cw_ref, cb_ref, wd_ref, fw_ref = refs[12:24]
    o_ref, u_s = refs[24:26]
    i = pl.program_id(1)
    n_i = pl.num_programs(1)
    hl = FFN_HALO
    acc = None
    for j in range(4):
        ycat = jnp.concatenate([r[...] for r in y_refs[3 * j:3 * j + 3]], axis=0)
        d = _dot(ycat, wo_ref[j * GROUP_W:(j + 1) * GROUP_W, :])
        acc = d if acc is None else acc + d
    x1 = jnp.concatenate([x_ref[...], xp_ref[...], xn_ref[...]], axis=0) + m_ref[2:3, :] * acc
    ms = jnp.mean(x1 * x1, axis=-1, keepdims=True)
    y = x1 * lax.rsqrt(ms + NORM_EPS) * nw_ref[...]
    h = (y * (1.0 + m_ref[4:5, :]) + m_ref[3:4, :]).astype(BF16)
    h_mid = h[0:tm, :]
    rid = lax.broadcasted_iota(jnp.int32, (tm, FFN_CHUNK), 0)
    has_prev = i > 0
    has_next = i < n_i - 1

    for j in range(D_FF // FFN_CHUNK):
        c0 = j * FFN_CHUNK
        a = _dot(h_mid, wa_ref[:, pl.ds(c0, FFN_CHUNK)])
        g = _dot(h, wg_ref[:, pl.ds(c0, FFN_CHUNK)])
        g_mid = g[0:tm, :]
        row_before = jnp.where(has_prev, g[tm + hl - 1:tm + hl, :], 0.0)
        row_after = jnp.where(has_next, g[tm + hl:tm + hl + 1, :], 0.0)
        g_prev = jnp.where(rid == 0, row_before, pltpu.roll(g_mid, 1, 0))
        g_next = jnp.where(rid == tm - 1, row_after, pltpu.roll(g_mid, tm - 1, 0))
        cw = cw_ref[:, pl.ds(c0, FFN_CHUNK)]
        gc = cw[0:1, :] * g_prev + cw[1:2, :] * g_mid + cw[2:3, :] * g_next + cb_ref[:, pl.ds(c0, FFN_CHUNK)]
        u_s[:, pl.ds(c0, FFN_CHUNK)] = (a * (gc * _sigmoid(gc))).astype(BF16)

    out = x1[0:tm, :] + m_ref[5:6, :] * _dot(u_s[...], wd_ref[...])
    if final_norm:
        ms2 = jnp.mean(out * out, axis=-1, keepdims=True)
        out = out * lax.rsqrt(ms2 + NORM_EPS) * fw_ref[...]
    o_ref[...] = out


def _ffn_call(ys, x, mods, layer, wo, norm_w, w_up, cw, cb, wd, fw, tm, is_ctx, final_norm):
    b, t, _ = x.shape
    n_tiles = t // tm
    hl = FFN_HALO
    hb = tm // hl
    row0 = SEQ if is_ctx else 0
    mi = 1 if is_ctx else 0

    def triple(width, rows, base):
        off_m = base // tm
        off_h = base // hl
        last = rows // hl - 1
        return [
            pl.BlockSpec((None, tm, width), lambda bi, i: (bi, i + off_m, 0)),
            pl.BlockSpec((None, hl, width), lambda bi, i: (bi, jnp.maximum(i * hb + off_h - 1, 0), 0)),
            pl.BlockSpec((None, hl, width), lambda bi, i: (bi, jnp.minimum((i + 1) * hb + off_h, last), 0)),
        ]

    in_specs = []
    args = []
    for yv in ys:
        in_specs += triple(GROUP_W, TT, row0)
        args += [yv, yv, yv]
    in_specs += triple(D_MODEL, t, 0)
    args += [x, x, x]
    in_specs += [
        pl.BlockSpec((None, None, 8, D_MODEL), lambda bi, i: (bi, mi, 0, 0)),
        _resident((None, 4 * GROUP_W, D_MODEL), (layer, 0, 0)),
        _resident((1, D_MODEL)),
        _resident((None, D_MODEL, D_FF), (layer, 0, 0)),
        _resident((None, D_MODEL, D_FF), (layer, 0, 1)),
        _resident((None, 3, D_FF), (layer, 0, 0)),
        _resident((None, 1, D_FF), (layer, 0, 0)),
        _resident((None, D_FF, D_MODEL), (layer, 0, 0)),
        _resident((1, D_MODEL)),
    ]
    args += [mods, wo, norm_w, w_up, w_up, cw, cb, wd, fw]
    return pl.pallas_call(
        functools.partial(_ffn_kernel, tm=tm, final_norm=final_norm),
        grid=(b, n_tiles),
        in_specs=in_specs,
        out_specs=pl.BlockSpec((None, tm, D_MODEL), lambda bi, i: (bi, i, 0)),
        out_shape=jax.ShapeDtypeStruct((b, t, D_MODEL), F32),
        scratch_shapes=[pltpu.VMEM((tm, D_FF), BF16)],
        compiler_params=_params(60),
        name="ffn_ctx" if is_ctx else "ffn_lat",
    )(*args)


def _pad_cols(w, width):
    return jnp.pad(w, ((0, 0), (0, width - w.shape[1])))


def _proj_weight(w_in):
    a_cols = 512
    b_cols = GROUP_W + SSM_CONV_CH + 8
    c_cols = 512
    w_in = w_in.astype(BF16)
    wa = w_in[..., :a_cols]
    wb = w_in[..., a_cols:a_cols + b_cols]
    wc = w_in[..., a_cols + b_cols:a_cols + b_cols + c_cols]
    wd = w_in[..., a_cols + b_cols + c_cols:]
    zeros = lambda n: jnp.zeros(w_in.shape[:-1] + (n,), w_in.dtype)
    w_zx = wb[..., :GROUP_W + SSM_CONV_CH]
    w_dt = jnp.concatenate([wb[..., GROUP_W + SSM_CONV_CH:], zeros(PDT_W - 8)], axis=-1)
    w_cq = jnp.concatenate([wd[..., :MLA_Q_RANK], zeros(2 * LANES - MLA_Q_RANK)], axis=-1)
    w_ckv = wd[..., MLA_Q_RANK:MLA_Q_RANK + MLA_KV_RANK]
    w_kr = jnp.concatenate([zeros(MLA_NOPE), wd[..., MLA_Q_RANK + MLA_KV_RANK:],
                            zeros(LANES - MLA_NOPE - MLA_ROPE)], axis=-1)
    return jnp.concatenate([wa, w_zx, w_dt, wc, w_cq, w_ckv, w_kr], axis=-1)


def _mla_weights(w_uq_l, w_ukv_l):
    dq = MLA_NOPE + MLA_ROPE
    wq = w_uq_l.reshape(MLA_Q_RANK, 4, dq)
    wq = jnp.pad(wq, ((0, 2 * LANES - MLA_Q_RANK), (0, 0), (0, LANES - dq))).reshape(2 * LANES, 4 * LANES)
    wkv = w_ukv_l.reshape(MLA_KV_RANK, 4, MLA_NOPE + HEAD_DIM)
    wk = jnp.pad(wkv[:, :, :MLA_NOPE], ((0, 0), (0, 0), (0, LANES - MLA_NOPE))).reshape(MLA_KV_RANK, 4 * LANES)
    wv = wkv[:, :, MLA_NOPE:].reshape(MLA_KV_RANK, 4 * HEAD_DIM)
    return wq.astype(BF16), jnp.concatenate([wk, wv], axis=1).astype(BF16)


def kernel(x, c, ctx, c_ctx, norm1_w, w_mod, b_mod, w_in, attn_q_norm, attn_k_norm, ssm_conv_w, ssm_conv_b,
           ssm_dt_bias, ssm_a_log, ssm_d, ssm_norm_w, win_sink, mla_q_norm, mla_w_uq, mla_kv_norm, mla_w_ukv,
           w_out, norm2_w, ffn_w_up, ffn_conv_w, ffn_conv_b, ffn_w_down, final_norm_w):
    bsz = x.shape[0]
    depth = w_in.shape[0]
    assert x.shape[1:] == (SEQ, D_MODEL) and ctx.shape[1:] == (CTX_LEN, D_MODEL)

    tables = _axial_tables(HEAD_DIM, 0, 2) + _axial_tables(MLA_ROPE, MLA_NOPE, 1)

    n_rows = ((bsz + 1 + 7) // 8) * 8
    cvec = jnp.concatenate([c, c_ctx[None, :], jnp.zeros((n_rows - bsz - 1, D_MODEL), F32)], axis=0)
    mods_all = _mod_call(cvec, w_mod, b_mod.reshape(depth, 1, 6 * D_MODEL))

    w_in_all = _proj_weight(w_in)
    w_o_all = w_out.astype(BF16)
    w_up_all = ffn_w_up.astype(BF16)
    wd_all = ffn_w_down.astype(BF16)
    cw_all = jnp.swapaxes(ffn_conv_w, 1, 2)
    cb_all = ffn_conv_b.reshape(depth, 1, D_FF)
    fw = final_norm_w.reshape(1, D_MODEL)

    x_lat, x_ctx = x, ctx
    for l in range(depth):
        with_ctx = l < depth - 1
        m_lat = mods_all[l, :bsz].reshape(bsz, 6, D_MODEL)
        m_ctx = jnp.broadcast_to(mods_all[l, bsz].reshape(1, 6, D_MODEL), (bsz, 6, D_MODEL))
        mods = jnp.pad(jnp.stack([m_lat, m_ctx], axis=1), ((0, 0), (0, 0), (0, 2), (0, 0)))

        wq, wkv = _mla_weights(mla_w_uq[l], mla_w_ukv[l])
        qa, ka, va, pb, pdt, qc, kc, vc, qd, kd, vd = _proj_call(
            x_lat, x_ctx, norm1_w[l].reshape(1, D_MODEL), mods, l, w_in_all, tables,
            jnp.tile(attn_q_norm[l], 2).reshape(1, LANES), jnp.tile(attn_k_norm[l], 2).reshape(1, LANES),
            _pad_cols(mla_q_norm[l].reshape(1, MLA_Q_RANK), 2 * LANES), mla_kv_norm[l].reshape(1, MLA_KV_RANK),
            wq, wkv)

        ya = _dense_attn_call(qa, ka, va, None, (0, 0, 0, 0), with_ctx, "attn_gqa")
        ym = _dense_attn_call(qd, kd, vd, None, (0, 1, 2, 3), with_ctx, "attn_mla")
        yw = _win_attn_call(qc, kc, vc, win_sink[l], with_ctx)
        yb = _ssd_call(
            pb, pdt, ssm_conv_w[l].T, ssm_conv_b[l].reshape(1, SSM_CONV_CH),
            jnp.tile(jnp.broadcast_to(ssm_dt_bias[l].reshape(8, 1), (8, SSD_CHUNK)), (TT // SSD_CHUNK, 1)),
            jnp.tile(jnp.broadcast_to(ssm_a_log[l].reshape(8, 1), (8, SSD_CHUNK)), (TT // SSD_CHUNK, 1)),
            jnp.repeat(ssm_d[l], HEAD_DIM).reshape(1, GROUP_W), ssm_norm_w[l].reshape(1, GROUP_W))

        ys = (ya, yb, yw, ym)
        n2 = norm2_w[l].reshape(1, D_MODEL)
        last = l == depth - 1
        ffn_w = (w_o_all, n2, w_up_all, cw_all, cb_all, wd_all, fw)
        x_lat_next = _ffn_call(ys, x_lat, mods, l, *ffn_w, FFN_TILE, False, last)
        if with_ctx:
            x_ctx = _ffn_call(ys, x_ctx, mods, l, *ffn_w, CTX_LEN, True, False)
        x_lat = x_lat_next
    return x_lat
```
